```python
import math
import jax, jax.numpy as jnp
from jax import lax
import numpy as np

D_MODEL = 1024
BATCH = 8
SEQ = 4096
DEPTH = 4

CHUNK = 64
Q_BLOCK = 128
EPS = 1e-6
ROPE_THETA = 10000.0

SB_HEAD_DIM = 64
SB_WIDTH = D_MODEL // 2
SB_HEADS = SB_WIDTH // SB_HEAD_DIM
DF_HEAD_DIM = 64
DF_WIDTH = D_MODEL - SB_WIDTH
DF_HEADS = DF_WIDTH // (2 * DF_HEAD_DIM)
MIX_WIDTH = SB_WIDTH + DF_WIDTH
PROJ_WIDTH = 3 * SB_WIDTH + 3 * DF_WIDTH
D_FF = 4 * D_MODEL

kernel_name = "hybrid_stickbreak_diffattn_sqrelu"


def rmsnorm(x, g):
    xf = x.astype(jnp.float32)
    y = xf * lax.rsqrt(jnp.mean(xf * xf, axis=-1, keepdims=True) + EPS) * g.astype(jnp.float32)
    return y.astype(x.dtype)


def rope(x, pos):
    d = x.shape[-1]
    inv = 1.0 / (ROPE_THETA ** (jnp.arange(0, d, 2, dtype=jnp.float32) / d))
    ang = pos.astype(jnp.float32)[:, None] * inv[None, :]
    ang = jnp.concatenate([ang, ang], axis=-1)
    shape = (1, x.shape[1]) + (1,) * (x.ndim - 3) + (d,)
    cos = jnp.cos(ang).reshape(shape)
    sin = jnp.sin(ang).reshape(shape)
    xf = x.astype(jnp.float32)
    x1, x2 = xf[..., : d // 2], xf[..., d // 2:]
    rot = jnp.concatenate([-x2, x1], axis=-1)
    return (xf * cos + rot * sin).astype(x.dtype)


def stick_breaking_attention(q, k, v):
    B, S, H, d = q.shape
    nb = S // Q_BLOCK
    scale = d ** -0.5
    kh = k.transpose(0, 2, 1, 3)
    vh = v.transpose(0, 2, 1, 3)
    qb = q.transpose(0, 2, 1, 3).reshape(B, H, nb, Q_BLOCK, d).transpose(2, 0, 1, 3, 4)
    kpos = jnp.arange(S)
    qpos = kpos.reshape(nb, Q_BLOCK)

    def block(args):
        qblk, qp = args
        z = jnp.einsum('bhqd,bhkd->bhqk', qblk, kh).astype(jnp.float32) * scale
        strict = kpos[None, :] < qp[:, None]
        log_beta = jax.nn.log_sigmoid(z)
        log_keep = jnp.where(strict, jax.nn.log_sigmoid(-z), 0.0)
        between = lax.cumsum(log_keep, axis=log_keep.ndim - 1, reverse=True) - log_keep
        w = jnp.where(strict, jnp.exp(log_beta + between), 0.0)
        return jnp.einsum('bhqk,bhkd->bhqd', w.astype(vh.dtype), vh)

    out = lax.map(block, (qb, qpos))
    return out.transpose(1, 0, 3, 2, 4).reshape(B, S, H * d)


def differential_attention(q, k, v, lam):
    B, S, H, _, d = q.shape
    dv = v.shape[-1]
    nb = S // Q_BLOCK
    scale = d ** -0.5
    kh = k.transpose(0, 2, 3, 1, 4)
    vh = v.transpose(0, 2, 1, 3)
    qb = q.transpose(0, 2, 3, 1, 4).reshape(B, H, 2, nb, Q_BLOCK, d).transpose(3, 0, 1, 2, 4, 5)
    kchunk = jnp.arange(S) // CHUNK
    qpos = jnp.arange(S).reshape(nb, Q_BLOCK)
    neg = jnp.finfo(jnp.float32).min

    def block(args):
        qblk, qp = args
        s = jnp.einsum('bhmqd,bhmkd->bhmqk', qblk, kh).astype(jnp.float32) * scale
        allowed = kchunk[None, :] <= (qp // CHUNK)[:, None]
        p = jax.nn.softmax(jnp.where(allowed, s, neg), axis=-1)
        attn = p[:, :, 0] - lam * p[:, :, 1]
        return jnp.einsum('bhqk,bhkd->bhqd', attn.astype(vh.dtype), vh)

    out = lax.map(block, (qb, qpos))
    return out.transpose(1, 0, 3, 2, 4).reshape(B, S, H, dv)


def setup_inputs(seed: int = 0) -> dict:
    key = jax.random.key(seed)
    ks = jax.random.split(key, 14)
    f32 = jnp.float32
    out_scale = (2.0 * DEPTH) ** -0.5
    x = jax.random.normal(ks[0], (BATCH, SEQ, D_MODEL), f32)
    w_in = jax.random.normal(ks[1], (DEPTH, D_MODEL, PROJ_WIDTH), f32) * D_MODEL ** -0.5
    w_o = jax.random.normal(ks[2], (DEPTH, MIX_WIDTH, D_MODEL), f32) * (MIX_WIDTH ** -0.5) * out_scale
    attn_norm = 1.0 + 0.02 * jax.random.normal(ks[3], (DEPTH, D_MODEL), f32)
    subln_norm = 1.0 + 0.02 * jax.random.normal(ks[4], (DEPTH, 2 * DF_HEAD_DIM), f32)
    lam_q1 = 0.1 * jax.random.normal(ks[5], (DEPTH, DF_HEAD_DIM), f32)
    lam_k1 = 0.1 * jax.random.normal(ks[6], (DEPTH, DF_HEAD_DIM), f32)
    lam_q2 = 0.1 * jax.random.normal(ks[7], (DEPTH, DF_HEAD_DIM), f32)
    lam_k2 = 0.1 * jax.random.normal(ks[8], (DEPTH, DF_HEAD_DIM), f32)
    mlp_norm = 1.0 + 0.02 * jax.random.normal(ks[9], (DEPTH, D_MODEL), f32)
    w_ff1 = jax.random.normal(ks[10], (DEPTH, D_MODEL, D_FF), f32) * D_MODEL ** -0.5
    w_ff2 = jax.random.normal(ks[11], (DEPTH, D_FF, D_MODEL), f32) * (D_FF ** -0.5) * out_scale
    final_norm = 1.0 + 0.02 * jax.random.normal(ks[12], (D_MODEL,), f32)
    return {"x": x, "w_in": w_in, "w_o": w_o, "attn_norm": attn_norm,
            "subln_norm": subln_norm, "lam_q1": lam_q1, "lam_k1": lam_k1,
            "lam_q2": lam_q2, "lam_k2": lam_k2, "mlp_norm": mlp_norm,
            "w_ff1": w_ff1, "w_ff2": w_ff2, "final_norm": final_norm}


def reference(x, w_in, w_o, attn_norm, subln_norm, lam_q1, lam_k1, lam_q2, lam_k2,
              mlp_norm, w_ff1, w_ff2, final_norm):
    B, S, _ = x.shape
    pos = jnp.arange(S)
    splits = np.cumsum([SB_WIDTH, SB_WIDTH, SB_WIDTH, DF_WIDTH, DF_WIDTH]).tolist()
    for l in range(DEPTH):
        h = rmsnorm(x, attn_norm[l])
        proj = jnp.einsum('bsd,de->bse', h, w_in[l])
        sb_q, sb_k, sb_v, df_q, df_k, df_v = jnp.split(proj, splits, axis=-1)

        sb_out = stick_breaking_attention(
            sb_q.reshape(B, S, SB_HEADS, SB_HEAD_DIM),
            sb_k.reshape(B, S, SB_HEADS, SB_HEAD_DIM),
            sb_v.reshape(B, S, SB_HEADS, SB_HEAD_DIM))

        lambda_init = 0.8 - 0.6 * math.exp(-0.3 * l)
        lam = (jnp.exp(jnp.sum(lam_q1[l].astype(jnp.float32) * lam_k1[l].astype(jnp.float32)))
               - jnp.exp(jnp.sum(lam_q2[l].astype(jnp.float32) * lam_k2[l].astype(jnp.float32)))
               + lambda_init)
        dq = rope(df_q.reshape(B, S, DF_HEADS, 2, DF_HEAD_DIM), pos)
        dk = rope(df_k.reshape(B, S, DF_HEADS, 2, DF_HEAD_DIM), pos)
        dvv = df_v.reshape(B, S, DF_HEADS, 2 * DF_HEAD_DIM)
        df_heads = differential_attention(dq, dk, dvv, lam)
        df_out = (rmsnorm(df_heads, subln_norm[l]) * (1.0 - lambda_init)).reshape(B, S, DF_WIDTH)

        mixed = jnp.concatenate([sb_out, df_out.astype(sb_out.dtype)], axis=-1)
        x = x + jnp.einsum('bse,ed->bsd', mixed, w_o[l])

        h = rmsnorm(x, mlp_norm[l])
        u = jnp.square(jax.nn.relu(jnp.einsum('bsd,df->bsf', h, w_ff1[l])))
        x = x + jnp.einsum('bsf,fd->bsd', u, w_ff2[l])
    return rmsnorm(x, final_norm)
```

```python
import functools
import math

import jax
import jax.numpy as jnp
from jax import lax
from jax.experimental import pallas as pl
from jax.experimental.pallas import tpu as pltpu

EPS = 1e-6
ROPE_THETA = 10000.0
HEAD_DIM = 64
LANES = 128
CHUNK = 64
VMEM_LIMIT = 56 * 1024 * 1024

PROJ_TM = 512
MLP_TM = 512
FF_CHUNK = 1024
SB_QB = 256
DF_QB = 256
SB_SKIP_BELOW = -90.0
NEG_BIG = -1e30


def _dot(a, b):
    return jnp.dot(a, b, preferred_element_type=jnp.float32)


def _dot_nt(a, b):
    return lax.dot_general(a, b, (((1,), (1,)), ((), ())),
                           preferred_element_type=jnp.float32)


def _rmsnorm(x, g):
    return x * lax.rsqrt(jnp.mean(x * x, axis=-1, keepdims=True) + EPS) * g


def _proj_kernel(x_ref, g_ref, w_ref, cos_ref, sin_lo_ref, sin_hi_ref, o_ref, *, width):
    h = _rmsnorm(x_ref[...], g_ref[...]).astype(jnp.bfloat16)
    scale = HEAD_DIM ** -0.5
    cos = cos_ref[...]
    sin_lo = sin_lo_ref[...]
    sin_hi = sin_hi_ref[...]
    for grp in range(6):
        p = _dot(h, w_ref[:, grp * width:(grp + 1) * width])
        rotary = grp in (3, 4)
        scaled = grp in (0, 3)
        for c in range(width // LANES):
            t = p[:, c * LANES:(c + 1) * LANES]
            if rotary:
                t = (t * cos + pltpu.roll(t, LANES - HEAD_DIM // 2, 1) * sin_lo
                     + pltpu.roll(t, HEAD_DIM // 2, 1) * sin_hi)
            if scaled:
                t = t * scale
            o_ref[:, grp * width + c * LANES: grp * width + (c + 1) * LANES] = t.astype(o_ref.dtype)


def _proj(x2d, g, w, cos, sin_lo, sin_hi, seq):
    n, d = x2d.shape
    pw = w.shape[1]
    tm = PROJ_TM
    n_pos_blocks = seq // tm
    tab_spec = pl.BlockSpec((tm, LANES), lambda i: (i % n_pos_blocks, 0))
    return pl.pallas_call(
        functools.partial(_proj_kernel, width=pw // 6),
        grid=(n // tm,),
        in_specs=[
            pl.BlockSpec((tm, d), lambda i: (i, 0)),
            pl.BlockSpec((1, d), lambda i: (0, 0)),
            pl.BlockSpec((d, pw), lambda i: (0, 0), pipeline_mode=pl.Buffered(1)),
            tab_spec, tab_spec, tab_spec,
        ],
        out_specs=pl.BlockSpec((tm, pw), lambda i: (i, 0)),
        out_shape=jax.ShapeDtypeStruct((n, pw), jnp.bfloat16),
        compiler_params=pltpu.CompilerParams(
            dimension_semantics=("parallel",), vmem_limit_bytes=VMEM_LIMIT),
        name="proj",
    )(x2d, g, w, cos, sin_lo, sin_hi)


def _split_halves(q):
    lane = lax.broadcasted_iota(jnp.int32, q.shape, 1)
    zero = jnp.zeros_like(q)
    return jnp.where(lane < HEAD_DIM, q, zero), jnp.where(lane >= HEAD_DIM, q, zero)


def _sb_kernel(q_ref, k_ref, v_ref, o_ref, acc_ref, carry_ref, *, blk):
    qi = pl.program_id(2)
    q_halves = _split_halves(q_ref[0])
    row = lax.broadcasted_iota(jnp.int32, (blk, blk), 0)
    col = lax.broadcasted_iota(jnp.int32, (blk, blk), 1)
    tri = (row > col).astype(jnp.bfloat16)
    strict = col < row

    acc_ref[...] = jnp.zeros_like(acc_ref)
    carry_ref[...] = jnp.zeros_like(carry_ref)

    def block(j, masked):
        start = pl.multiple_of(j * blk, blk)
        k = k_ref[0, pl.ds(start, blk), :]
        v = v_ref[0, pl.ds(start, blk), :]
        for h in range(2):
            z = _dot_nt(q_halves[h], k)
            sp = jnp.maximum(z, 0.0) + jnp.log1p(jnp.exp(-jnp.abs(z)))
            log_keep = -sp
            log_beta = z - sp
            if masked:
                log_keep = jnp.where(strict, log_keep, 0.0)
            hi = log_keep.astype(jnp.bfloat16)
            lo = (log_keep - hi.astype(jnp.float32)).astype(jnp.bfloat16)
            suffix = _dot(hi, tri) + _dot(lo, tri)
            carry = carry_ref[h]
            w = jnp.exp(log_beta + suffix + carry)
            if masked:
                w = jnp.where(strict, w, 0.0)
            acc_ref[h] += _dot(w.astype(jnp.bfloat16), v)
            carry_ref[h] = carry + jnp.sum(log_keep, axis=-1, keepdims=True)

    block(qi, True)

    def cond(state):
        j, max_carry = state
        return jnp.logical_and(j >= 0, max_carry > SB_SKIP_BELOW)

    def body(state):
        j, _ = state
        block(j, False)
        return j - 1, jnp.max(carry_ref[...])

    lax.while_loop(cond, body, (qi - 1, jnp.max(carry_ref[...])))

    lane = lax.broadcasted_iota(jnp.int32, (blk, LANES), 1)
    o_ref[0] = jnp.where(lane < HEAD_DIM, acc_ref[0], acc_ref[1]).astype(o_ref.dtype)


def _sb_attention(proj3, mixed_cols):
    b, s, _ = proj3.shape
    blk = SB_QB
    n_pairs = mixed_cols // 2 // LANES
    return pl.pallas_call(
        functools.partial(_sb_kernel, blk=blk),
        grid=(b, n_pairs, s // blk),
        in_specs=[
            pl.BlockSpec((1, blk, LANES), lambda bi, hp, qi: (bi, qi, hp)),
            pl.BlockSpec((1, s, LANES), lambda bi, hp, qi: (bi, 0, n_pairs + hp)),
            pl.BlockSpec((1, s, LANES), lambda bi, hp, qi: (bi, 0, 2 * n_pairs + hp)),
        ],
        out_specs=pl.BlockSpec((1, blk, LANES), lambda bi, hp, qi: (bi, qi, hp)),
        out_shape=jax.ShapeDtypeStruct((b, s, n_pairs * LANES), jnp.bfloat16),
        scratch_shapes=[
            pltpu.VMEM((2, blk, LANES), jnp.float32),
            pltpu.VMEM((2, blk, 1), jnp.float32),
        ],
        compiler_params=pltpu.CompilerParams(
            dimension_semantics=("parallel", "parallel", "arbitrary"),
            vmem_limit_bytes=VMEM_LIMIT),
        name="sb_attn",
    )(proj3, proj3, proj3)


def _df_kernel(q_ref, k_ref, v_ref, lq1_ref, lk1_ref, lq2_ref, lk2_ref, g_ref, o_ref,
               acc_ref, m_ref, l_ref, *, blk, lambda_init):
    qi = pl.program_id(2)
    q_halves = _split_halves(q_ref[0])
    row = lax.broadcasted_iota(jnp.int32, (blk, blk), 0)
    col = lax.broadcasted_iota(jnp.int32, (blk, blk), 1)
    allowed = (col // CHUNK) <= (row // CHUNK)

    acc_ref[...] = jnp.zeros_like(acc_ref)
    l_ref[...] = jnp.zeros_like(l_ref)
    m_ref[...] = jnp.full_like(m_ref, NEG_BIG)

    def block(j, masked):
        start = pl.multiple_of(j * blk, blk)
        k = k_ref[0, pl.ds(start, blk), :]
        v = v_ref[0, pl.ds(start, blk), :]
        for h in range(2):
            s = _dot_nt(q_halves[h], k)
            if masked:
                s = jnp.where(allowed, s, NEG_BIG)
            m_old = m_ref[h]
            m_new = jnp.maximum(m_old, jnp.max(s, axis=-1, keepdims=True))
            alpha = jnp.exp(m_old - m_new)
            p = jnp.exp(s - m_new)
            l_ref[h] = alpha * l_ref[h] + jnp.sum(p, axis=-1, keepdims=True)
            acc_ref[h] = alpha * acc_ref[h] + _dot(p.astype(jnp.bfloat16), v)
            m_ref[h] = m_new

    def body(j, c):
        block(j, False)
        return c

    lax.fori_loop(0, qi, body, 0)
    block(qi, True)

    lam = (jnp.exp(jnp.sum(lq1_ref[...] * lk1_ref[...]))
           - jnp.exp(jnp.sum(lq2_ref[...] * lk2_ref[...])) + lambda_init)
    out = acc_ref[0] / l_ref[0] - lam * (acc_ref[1] / l_ref[1])
    out = _rmsnorm(out, g_ref[...]) * (1.0 - lambda_init)
    o_ref[0] = out.astype(o_ref.dtype)


def _df_attention(proj3, lq1, lk1, lq2, lk2, g, mixed_cols, lambda_init):
    b, s, _ = proj3.shape
    blk = DF_QB
    n_heads = mixed_cols // 2 // LANES
    base = 3 * n_heads
    vec = pl.BlockSpec((1, HEAD_DIM), lambda bi, h, qi: (0, 0))
    return pl.pallas_call(
        functools.partial(_df_kernel, blk=blk, lambda_init=lambda_init),
        grid=(b, n_heads, s // blk),
        in_specs=[
            pl.BlockSpec((1, blk, LANES), lambda bi, h, qi: (bi, qi, base + h)),
            pl.BlockSpec((1, s, LANES), lambda bi, h, qi: (bi, 0, base + n_heads + h)),
            pl.BlockSpec((1, s, LANES), lambda bi, h, qi: (bi, 0, base + 2 * n_heads + h)),
            vec, vec, vec, vec,
            pl.BlockSpec((1, LANES), lambda bi, h, qi: (0, 0)),
        ],
        out_specs=pl.BlockSpec((1, blk, LANES), lambda bi, h, qi: (bi, qi, h)),
        out_shape=jax.ShapeDtypeStruct((b, s, n_heads * LANES), jnp.bfloat16),
        scratch_shapes=[
            pltpu.VMEM((2, blk, LANES), jnp.float32),
            pltpu.VMEM((2, blk, 1), jnp.float32),
            pltpu.VMEM((2, blk, 1), jnp.float32),
        ],
        compiler_params=pltpu.CompilerParams(
            dimension_semantics=("parallel", "parallel", "arbitrary"),
            vmem_limit_bytes=VMEM_LIMIT),
        name="df_attn",
    )(proj3, proj3, proj3, lq1, lk1, lq2, lk2, g)


def _mlp_kernel(x_ref, sb_ref, df_ref, wo_ref, g_ref, w1_ref, w2_ref, gf_ref, o_ref, *, final):
    half = sb_ref.shape[1]
    x1 = (x_ref[...] + _dot(sb_ref[...], wo_ref[:half, :]) + _dot(df_ref[...], wo_ref[half:, :]))
    h = _rmsnorm(x1, g_ref[...]).astype(jnp.bfloat16)
    y = None
    for c in range(w1_ref.shape[1] // FF_CHUNK):
        u = _dot(h, w1_ref[:, c * FF_CHUNK:(c + 1) * FF_CHUNK])
        u = jnp.square(jnp.maximum(u, 0.0)).astype(jnp.bfloat16)
        t = _dot(u, w2_ref[c * FF_CHUNK:(c + 1) * FF_CHUNK, :])
        y = t if y is None else y + t
    out = x1 + y
    if final:
        out = _rmsnorm(out, gf_ref[...])
    o_ref[...] = out


def _mlp(x2d, sb2d, df2d, wo, g, w1, w2, gf, final):
    n, d = x2d.shape
    half = sb2d.shape[1]
    ff = w1.shape[1]
    tm = MLP_TM
    const = lambda i: (0, 0)
    resident = dict(pipeline_mode=pl.Buffered(1))
    return pl.pallas_call(
        functools.partial(_mlp_kernel, final=final),
        grid=(n // tm,),
        in_specs=[
            pl.BlockSpec((tm, d), lambda i: (i, 0)),
            pl.BlockSpec((tm, half), lambda i: (i, 0)),
            pl.BlockSpec((tm, half), lambda i: (i, 0)),
            pl.BlockSpec((2 * half, d), const, **resident),
            pl.BlockSpec((1, d), const),
            pl.BlockSpec((d, ff), const, **resident),
            pl.BlockSpec((ff, d), const, **resident),
            pl.BlockSpec((1, d), const),
        ],
        out_specs=pl.BlockSpec((tm, d), lambda i: (i, 0)),
        out_shape=jax.ShapeDtypeStruct((n, d), jnp.float32),
        compiler_params=pltpu.CompilerParams(
            dimension_semantics=("parallel",), vmem_limit_bytes=VMEM_LIMIT),
        name="mlp",
    )(x2d, sb2d, df2d, wo, g, w1, w2, gf)


def _rope_tables(seq):
    half = HEAD_DIM // 2
    inv = 1.0 / (ROPE_THETA ** (jnp.arange(0, HEAD_DIM, 2, dtype=jnp.float32) / HEAD_DIM))
    ang = jnp.arange(seq, dtype=jnp.float32)[:, None] * inv[None, :]
    ang = jnp.concatenate([ang, ang], axis=-1)
    cos = jnp.cos(ang)
    sin = jnp.sin(ang)
    lo = jnp.arange(HEAD_DIM) < half
    sin_lo = jnp.where(lo, -sin, 0.0)
    sin_hi = jnp.where(lo, 0.0, sin)
    rep = LANES // HEAD_DIM
    return tuple(jnp.tile(t, (1, rep)) for t in (cos, sin_lo, sin_hi))


def kernel(x, w_in, w_o, attn_norm, subln_norm, lam_q1, lam_k1, lam_q2, lam_k2, mlp_norm, w_ff1, w_ff2, final_norm):
    b, s, d = x.shape
    depth = w_in.shape[0]
    mixed_cols = w_o.shape[1]
    assert s % PROJ_TM == 0 and (b * s) % MLP_TM == 0
    assert s % SB_QB == 0 and s % DF_QB == 0 and DF_QB % CHUNK == 0
    assert w_in.shape[2] == 3 * mixed_cols and subln_norm.shape[1] == LANES

    cos, sin_lo, sin_hi = _rope_tables(s)
    bf = jnp.bfloat16
    x2d = x.reshape(b * s, d)
    for l in range(depth):
        lambda_init = 0.8 - 0.6 * math.exp(-0.3 * l)
        proj = _proj(x2d, attn_norm[l][None, :], w_in[l].astype(bf), cos, sin_lo, sin_hi, s)
        proj3 = proj.reshape(b, s, -1)
        sb = _sb_attention(proj3, mixed_cols)
        df = _df_attention(proj3, lam_q1[l][None, :], lam_k1[l][None, :], lam_q2[l][None, :],
                           lam_k2[l][None, :], subln_norm[l][None, :], mixed_cols, lambda_init)
        x2d = _mlp(x2d, sb.reshape(b * s, -1), df.reshape(b * s, -1), w_o[l].astype(bf),
                   mlp_norm[l][None, :], w_ff1[l].astype(bf), w_ff2[l].astype(bf),
                   final_norm[None, :], final=(l == depth - 1))
    return x2d.reshape(b, s, d)
```

```python
import functools
import math

import jax
import jax.numpy as jnp
from jax import lax
from jax.experimental import pallas as pl
from jax.experimental.pallas import tpu as pltpu

EPS = 1e-6
ROPE_THETA = 10000.0
HEAD_DIM = 64
LANES = 128
CHUNK = 64
VMEM_LIMIT = 56 * 1024 * 1024

PROJ_TM = 512
MLP_TM = 512
FF_CHUNK = 1024
SB_QB = 256
DF_QB = 512
SB_SKIP_BELOW = -90.0
NEG_BIG = -1e30


def _dot(a, b):
    return jnp.dot(a, b, preferred_element_type=jnp.float32)


def _dot_nt(a, b):
    return lax.dot_general(a, b, (((1,), (1,)), ((), ())),
                           preferred_element_type=jnp.float32)


def _dot_tn(a, b):
    return lax.dot_general(a, b, (((0,), (0,)), ((), ())),
                           preferred_element_type=jnp.float32)


def _rmsnorm(x, g):
    return x * lax.rsqrt(jnp.mean(x * x, axis=-1, keepdims=True) + EPS) * g


def _proj_kernel(x_ref, g_ref, w_ref, cos_ref, sin_lo_ref, sin_hi_ref, o_ref, *, width):
    h = _rmsnorm(x_ref[...], g_ref[...]).astype(jnp.bfloat16)
    scale = HEAD_DIM ** -0.5
    cos = cos_ref[...]
    sin_lo = sin_lo_ref[...]
    sin_hi = sin_hi_ref[...]
    for grp in range(6):
        p = _dot(h, w_ref[:, grp * width:(grp + 1) * width])
        rotary = grp in (3, 4)
        scaled = grp in (0, 3)
        for c in range(width // LANES):
            t = p[:, c * LANES:(c + 1) * LANES]
            if rotary:
                t = (t * cos + pltpu.roll(t, LANES - HEAD_DIM // 2, 1) * sin_lo
                     + pltpu.roll(t, HEAD_DIM // 2, 1) * sin_hi)
            if scaled:
                t = t * scale
            o_ref[:, grp * width + c * LANES: grp * width + (c + 1) * LANES] = t.astype(o_ref.dtype)


def _proj(x2d, g, w, cos, sin_lo, sin_hi, seq):
    n, d = x2d.shape
    pw = w.shape[1]
    tm = PROJ_TM
    n_pos_blocks = seq // tm
    tab_spec = pl.BlockSpec((tm, LANES), lambda i: (i % n_pos_blocks, 0))
    return pl.pallas_call(
        functools.partial(_proj_kernel, width=pw // 6),
        grid=(n // tm,),
        in_specs=[
            pl.BlockSpec((tm, d), lambda i: (i, 0)),
            pl.BlockSpec((1, d), lambda i: (0, 0)),
            pl.BlockSpec((d, pw), lambda i: (0, 0), pipeline_mode=pl.Buffered(1)),
            tab_spec, tab_spec, tab_spec,
        ],
        out_specs=pl.BlockSpec((tm, pw), lambda i: (i, 0)),
        out_shape=jax.ShapeDtypeStruct((n, pw), jnp.bfloat16),
        compiler_params=pltpu.CompilerParams(
            dimension_semantics=("parallel",), vmem_limit_bytes=VMEM_LIMIT),
        name="proj",
    )(x2d, g, w, cos, sin_lo, sin_hi)


def _split_halves(q):
    lane = lax.broadcasted_iota(jnp.int32, q.shape, 1)
    zero = jnp.zeros_like(q)
    return jnp.where(lane < HEAD_DIM, q, zero), jnp.where(lane >= HEAD_DIM, q, zero)


def _sb_kernel(q_ref, k_ref, v_ref, o_ref, acc_ref, carry_ref, *, blk):
    qi = pl.program_id(2)
    q_halves = _split_halves(q_ref[0])
    key = lax.broadcasted_iota(jnp.int32, (blk, blk), 0)
    qry = lax.broadcasted_iota(jnp.int32, (blk, blk), 1)
    tri = (qry > key).astype(jnp.bfloat16)
    strict = key < qry

    def load_kv(j):
        start = pl.multiple_of(j * blk, blk)
        return k_ref[0, pl.ds(start, blk), :], v_ref[0, pl.ds(start, blk), :]

    def tile(k, v, q_half, carry, masked):
        z = _dot_nt(k, q_half)
        log_keep = jnp.minimum(-z, 0.0) - jnp.log(1.0 + jnp.exp(-jnp.abs(z)))
        log_beta = z + log_keep
        if masked:
            log_keep = jnp.where(strict, log_keep, 0.0)
        hi = log_keep.astype(jnp.bfloat16)
        lo = (log_keep - hi.astype(jnp.float32)).astype(jnp.bfloat16)
        suffix = _dot(tri, hi) + _dot(tri, lo)
        arg = log_beta + suffix
        if carry is not None:
            arg = arg + carry
        w = jnp.exp(arg)
        if masked:
            w = jnp.where(strict, w, 0.0)
        return _dot_tn(v, w.astype(jnp.bfloat16)), jnp.sum(log_keep, axis=0, keepdims=True)

    k_diag, v_diag = load_kv(qi)
    k_left, v_left = load_kv(jnp.maximum(qi - 1, 0))
    left_bias = jnp.where(qi > 0, 0.0, NEG_BIG)
    for h in range(2):
        out_diag, sum_diag = tile(k_diag, v_diag, q_halves[h], None, True)
        out_left, sum_left = tile(k_left, v_left, q_halves[h], sum_diag + left_bias, False)
        acc_ref[h] = out_diag + out_left
        carry_ref[h] = sum_diag + sum_left

    def cond(state):
        j, max_carry = state
        return jnp.logical_and(j >= 0, max_carry > SB_SKIP_BELOW)

    def body(state):
        j, _ = state
        k, v = load_kv(j)
        for h in range(2):
            carry = carry_ref[h]
            out, col_sum = tile(k, v, q_halves[h], carry, False)
            acc_ref[h] += out
            carry_ref[h] = carry + col_sum
        return j - 1, jnp.max(carry_ref[...])

    lax.while_loop(cond, body, (qi - 2, jnp.max(carry_ref[...])))

    dim = lax.broadcasted_iota(jnp.int32, (LANES, blk), 0)
    out_t = jnp.where(dim < HEAD_DIM, acc_ref[0], acc_ref[1])
    o_ref[0] = out_t.T.astype(o_ref.dtype)


def _sb_attention(proj3, mixed_cols):
    b, s, _ = proj3.shape
    blk = SB_QB
    n_pairs = mixed_cols // 2 // LANES
    return pl.pallas_call(
        functools.partial(_sb_kernel, blk=blk),
        grid=(b, n_pairs, s // blk),
        in_specs=[
            pl.BlockSpec((1, blk, LANES), lambda bi, hp, qi: (bi, qi, hp)),
            pl.BlockSpec((1, s, LANES), lambda bi, hp, qi: (bi, 0, n_pairs + hp)),
            pl.BlockSpec((1, s, LANES), lambda bi, hp, qi: (bi, 0, 2 * n_pairs + hp)),
        ],
        out_specs=pl.BlockSpec((1, blk, LANES), lambda bi, hp, qi: (bi, qi, hp)),
        out_shape=jax.ShapeDtypeStruct((b, s, n_pairs * LANES), jnp.bfloat16),
        scratch_shapes=[
            pltpu.VMEM((2, LANES, blk), jnp.float32),
            pltpu.VMEM((2, 1, blk), jnp.float32),
        ],
        compiler_params=pltpu.CompilerParams(
            dimension_semantics=("parallel", "parallel", "arbitrary"),
            vmem_limit_bytes=VMEM_LIMIT),
        name="sb_attn",
    )(proj3, proj3, proj3)


def _df_kernel(q_ref, k_ref, v_ref, lq1_ref, lk1_ref, lq2_ref, lk2_ref, g_ref, o_ref,
               acc_ref, m_ref, l_ref, *, blk, lambda_init):
    qi = pl.program_id(2)
    q_halves = _split_halves(q_ref[0])
    key = lax.broadcasted_iota(jnp.int32, (blk, blk), 0)
    qry = lax.broadcasted_iota(jnp.int32, (blk, blk), 1)
    allowed = (key // CHUNK) <= (qry // CHUNK)

    acc_ref[...] = jnp.zeros_like(acc_ref)
    l_ref[...] = jnp.zeros_like(l_ref)
    m_ref[...] = jnp.full_like(m_ref, NEG_BIG)

    def block(j, masked):
        start = pl.multiple_of(j * blk, blk)
        k = k_ref[0, pl.ds(start, blk), :]
        v = v_ref[0, pl.ds(start, blk), :]
        for h in range(2):
            s = _dot_nt(k, q_halves[h])
            if masked:
                s = jnp.where(allowed, s, NEG_BIG)
            m_old = m_ref[h]
            m_new = jnp.maximum(m_old, jnp.max(s, axis=0, keepdims=True))
            alpha = jnp.exp(m_old - m_new)
            p = jnp.exp(s - m_new)
            l_ref[h] = alpha * l_ref[h] + jnp.sum(p, axis=0, keepdims=True)
            acc_ref[h] = alpha * acc_ref[h] + _dot_tn(v, p.astype(jnp.bfloat16))
            m_ref[h] = m_new

    def body(j, c):
        block(j, False)
        return c

    lax.fori_loop(0, qi, body, 0)
    block(qi, True)

    lam = (jnp.exp(jnp.sum(lq1_ref[...] * lk1_ref[...]))
           - jnp.exp(jnp.sum(lq2_ref[...] * lk2_ref[...])) + lambda_init)
    out_t = acc_ref[0] / l_ref[0] - lam * (acc_ref[1] / l_ref[1])
    ms = jnp.mean(out_t * out_t, axis=0, keepdims=True)
    out_t = out_t * lax.rsqrt(ms + EPS) * g_ref[...] * (1.0 - lambda_init)
    o_ref[0] = out_t.T.astype(o_ref.dtype)


def _df_attention(proj3, lq1, lk1, lq2, lk2, g_col, mixed_cols, lambda_init):
    b, s, _ = proj3.shape
    blk = DF_QB
    n_heads = mixed_cols // 2 // LANES
    base = 3 * n_heads
    vec = pl.BlockSpec((1, HEAD_DIM), lambda bi, h, qi: (0, 0))
    return pl.pallas_call(
        functools.partial(_df_kernel, blk=blk, lambda_init=lambda_init),
        grid=(b, n_heads, s // blk),
        in_specs=[
            pl.BlockSpec((1, blk, LANES), lambda bi, h, qi: (bi, qi, base + h)),
            pl.BlockSpec((1, s, LANES), lambda bi, h, qi: (bi, 0, base + n_heads + h)),
            pl.BlockSpec((1, s, LANES), lambda bi, h, qi: (bi, 0, base + 2 * n_heads + h)),
            vec, vec, vec, vec,
            pl.BlockSpec((LANES, 1), lambda bi, h, qi: (0, 0)),
        ],
        out_specs=pl.BlockSpec((1, blk, LANES), lambda bi, h, qi: (bi, qi, h)),
        out_shape=jax.ShapeDtypeStruct((b, s, n_heads * LANES), jnp.bfloat16),
        scratch_shapes=[
            pltpu.VMEM((2, LANES, blk), jnp.float32),
            pltpu.VMEM((2, 1, blk), jnp.float32),
            pltpu.VMEM((2, 1, blk), jnp.float32),
        ],
        compiler_params=pltpu.CompilerParams(
            dimension_semantics=("parallel", "parallel", "arbitrary"),
            vmem_limit_bytes=VMEM_LIMIT),
        name="df_attn",
    )(proj3, proj3, proj3, lq1, lk1, lq2, lk2, g_col)


def _mlp_kernel(x_ref, sb_ref, df_ref, wo_ref, g_ref, w1_ref, w2_ref, gf_ref, o_ref, *, final):
    half = sb_ref.shape[1]
    x1 = (x_ref[...] + _dot(sb_ref[...], wo_ref[:half, :]) + _dot(df_ref[...], wo_ref[half:, :]))
    h = _rmsnorm(x1, g_ref[...]).astype(jnp.bfloat16)
    y = None
    for c in range(w1_ref.shape[1] // FF_CHUNK):
        u = _dot(h, w1_ref[:, c * FF_CHUNK:(c + 1) * FF_CHUNK])
        u = jnp.square(jnp.maximum(u, 0.0)).astype(jnp.bfloat16)
        t = _dot(u, w2_ref[c * FF_CHUNK:(c + 1) * FF_CHUNK, :])
        y = t if y is None else y + t
    out = x1 + y
    if final:
        out = _rmsnorm(out, gf_ref[...])
    o_ref[...] = out


def _mlp(x2d, sb2d, df2d, wo, g, w1, w2, gf, final):
    n, d = x2d.shape
    half = sb2d.shape[1]
    ff = w1.shape[1]
    tm = MLP_TM
    const = lambda i: (0, 0)
    resident = dict(pipeline_mode=pl.Buffered(1))
    return pl.pallas_call(
        functools.partial(_mlp_kernel, final=final),
        grid=(n // tm,),
        in_specs=[
            pl.BlockSpec((tm, d), lambda i: (i, 0)),
            pl.BlockSpec((tm, half), lambda i: (i, 0)),
            pl.BlockSpec((tm, half), lambda i: (i, 0)),
            pl.BlockSpec((2 * half, d), const, **resident),
            pl.BlockSpec((1, d), const),
            pl.BlockSpec((d, ff), const, **resident),
            pl.BlockSpec((ff, d), const, **resident),
            pl.BlockSpec((1, d), const),
        ],
        out_specs=pl.BlockSpec((tm, d), lambda i: (i, 0)),
        out_shape=jax.ShapeDtypeStruct((n, d), jnp.float32),
        compiler_params=pltpu.CompilerParams(
            dimension_semantics=("parallel",), vmem_limit_bytes=VMEM_LIMIT),
        name="mlp",
    )(x2d, sb2d, df2d, wo, g, w1, w2, gf)


def _rope_tables(seq):
    half = HEAD_DIM // 2
    inv = 1.0 / (ROPE_THETA ** (jnp.arange(0, HEAD_DIM, 2, dtype=jnp.float32) / HEAD_DIM))
    ang = jnp.arange(seq, dtype=jnp.float32)[:, None] * inv[None, :]
    ang = jnp.concatenate([ang, ang], axis=-1)
    cos = jnp.cos(ang)
    sin = jnp.sin(ang)
    lo = jnp.arange(HEAD_DIM) < half
    sin_lo = jnp.where(lo, -sin, 0.0)
    sin_hi = jnp.where(lo, 0.0, sin)
    rep = LANES // HEAD_DIM
    return tuple(jnp.tile(t, (1, rep)) for t in (cos, sin_lo, sin_hi))


def kernel(x, w_in, w_o, attn_norm, subln_norm, lam_q1, lam_k1, lam_q2, lam_k2, mlp_norm, w_ff1, w_ff2, final_norm):
    b, s, d = x.shape
    depth = w_in.shape[0]
    mixed_cols = w_o.shape[1]
    assert s % PROJ_TM == 0 and (b * s) % MLP_TM == 0
    assert s % SB_QB == 0 and s % DF_QB == 0 and DF_QB % CHUNK == 0
    assert w_in.shape[2] == 3 * mixed_cols and subln_norm.shape[1] == LANES

    cos, sin_lo, sin_hi = _rope_tables(s)
    bf = jnp.bfloat16
    x2d = x.reshape(b * s, d)
    for l in range(depth):
        lambda_init = 0.8 - 0.6 * math.exp(-0.3 * l)
        proj = _proj(x2d, attn_norm[l][None, :], w_in[l].astype(bf), cos, sin_lo, sin_hi, s)
        proj3 = proj.reshape(b, s, -1)
        sb = _sb_attention(proj3, mixed_cols)
        df = _df_attention(proj3, lam_q1[l][None, :], lam_k1[l][None, :], lam_q2[l][None, :],
                           lam_k2[l][None, :], subln_norm[l][:, None], mixed_cols, lambda_init)
        x2d = _mlp(x2d, sb.reshape(b * s, -1), df.reshape(b * s, -1), w_o[l].astype(bf),
                   mlp_norm[l][None, :], w_ff1[l].astype(bf), w_ff2[l].astype(bf),
                   final_norm[None, :], final=(l == depth - 1))
    return x2d.reshape(b, s, d)
```

```python
import functools
import math

import jax
import jax.numpy as jnp
from jax import lax
from jax.experimental import pallas as pl
from jax.experimental.pallas import tpu as pltpu

EPS = 1e-6
ROPE_THETA = 10000.0
HEAD_DIM = 64
LANES = 128
CHUNK = 64
VMEM_LIMIT = 56 * 1024 * 1024

PROJ_TM = 512
MLP_TM = 512
FF_CHUNK = 1024
SB_QB = 256
DF_QB = 512
LOG2E = 1.4426950408889634
SB_SKIP_BELOW_LOG2 = -130.0
SUM_ROWS = 16
NEG_BIG = -1e30


def _dot(a, b):
    return jnp.dot(a, b, preferred_element_type=jnp.float32)


def _dot_nt(a, b):
    return lax.dot_general(a, b, (((1,), (1,)), ((), ())),
                           preferred_element_type=jnp.float32)


def _dot_tn(a, b):
    return lax.dot_general(a, b, (((0,), (0,)), ((), ())),
                           preferred_element_type=jnp.float32)


def _rmsnorm(x, g):
    return x * lax.rsqrt(jnp.mean(x * x, axis=-1, keepdims=True) + EPS) * g


def _proj_kernel(x_ref, g_ref, w_ref, cos_ref, sin_lo_ref, sin_hi_ref, o_ref, *, width):
    h = _rmsnorm(x_ref[...], g_ref[...]).astype(jnp.bfloat16)
    scale = HEAD_DIM ** -0.5
    cos = cos_ref[...]
    sin_lo = sin_lo_ref[...]
    sin_hi = sin_hi_ref[...]
    for grp in range(6):
        p = _dot(h, w_ref[:, grp * width:(grp + 1) * width])
        rotary = grp in (3, 4)
        scaled = grp in (0, 3)
        for c in range(width // LANES):
            t = p[:, c * LANES:(c + 1) * LANES]
            if rotary:
                t = (t * cos + pltpu.roll(t, LANES - HEAD_DIM // 2, 1) * sin_lo
                     + pltpu.roll(t, HEAD_DIM // 2, 1) * sin_hi)
            if scaled:
                t = t * scale
            o_ref[:, grp * width + c * LANES: grp * width + (c + 1) * LANES] = t.astype(o_ref.dtype)


def _proj(x2d, g, w, cos, sin_lo, sin_hi, seq):
    n, d = x2d.shape
    pw = w.shape[1]
    tm = PROJ_TM
    n_pos_blocks = seq // tm
    tab_spec = pl.BlockSpec((tm, LANES), lambda i: (i % n_pos_blocks, 0))
    return pl.pallas_call(
        functools.partial(_proj_kernel, width=pw // 6),
        grid=(n // tm,),
        in_specs=[
            pl.BlockSpec((tm, d), lambda i: (i, 0)),
            pl.BlockSpec((1, d), lambda i: (0, 0)),
            pl.BlockSpec((d, pw), lambda i: (0, 0), pipeline_mode=pl.Buffered(1)),
            tab_spec, tab_spec, tab_spec,
        ],
        out_specs=pl.BlockSpec((tm, pw), lambda i: (i, 0)),
        out_shape=jax.ShapeDtypeStruct((n, pw), jnp.bfloat16),
        compiler_params=pltpu.CompilerParams(
            dimension_semantics=("parallel",), vmem_limit_bytes=VMEM_LIMIT),
        name="proj",
    )(x2d, g, w, cos, sin_lo, sin_hi)


def _split_halves(q):
    lane = lax.broadcasted_iota(jnp.int32, q.shape, 1)
    zero = jnp.zeros_like(q)
    return jnp.where(lane < HEAD_DIM, q, zero), jnp.where(lane >= HEAD_DIM, q, zero)


def _sb_kernel(q_ref, k_ref, v_ref, tri_ref, o_ref, acc_ref, carry_ref, *, blk):
    qi = pl.program_id(2)
    q_a, q_b = _split_halves(q_ref[0])
    q2 = -jnp.concatenate([q_a, q_b], axis=0)
    key = lax.broadcasted_iota(jnp.int32, (blk, 2 * blk), 0)
    qry = lax.broadcasted_iota(jnp.int32, (blk, 2 * blk), 1) & (blk - 1)
    strict = key < qry

    def scores(start, n_keys):
        k = k_ref[0, pl.ds(start, n_keys), :]
        return _dot_nt(k, q2) * LOG2E

    def weights(zn2, carry, masked):
        log_keep = jnp.minimum(zn2, 0.0) - jnp.log2(1.0 + jnp.exp2(-jnp.abs(zn2)))
        log_beta = log_keep - zn2
        if masked:
            log_keep = jnp.where(strict, log_keep, 0.0)
        hi = log_keep.astype(jnp.bfloat16)
        lo = (log_keep - hi.astype(jnp.float32)).astype(jnp.bfloat16)
        sums = _dot(tri_ref[...], jnp.concatenate([hi, lo], axis=0))
        arg = log_beta + sums[:blk]
        if carry is not None:
            arg = arg + carry
        w = jnp.exp2(arg)
        if masked:
            w = jnp.where(strict, w, 0.0)
        return w.astype(jnp.bfloat16), sums[blk:blk + 1]

    @pl.when(qi == 0)
    def _():
        w, col_sum = weights(scores(0, blk), None, True)
        acc_ref[...] = _dot_tn(v_ref[0, pl.ds(0, blk), :], w)
        carry_ref[...] = col_sum

    @pl.when(qi > 0)
    def _():
        start = pl.multiple_of((qi - 1) * blk, blk)
        zn2 = scores(start, 2 * blk)
        w_diag, sum_diag = weights(zn2[blk:], None, True)
        w_left, sum_left = weights(zn2[:blk], sum_diag, False)
        w = jnp.concatenate([w_left, w_diag], axis=0)
        acc_ref[...] = _dot_tn(v_ref[0, pl.ds(start, 2 * blk), :], w)
        carry_ref[...] = sum_diag + sum_left

    def cond(state):
        j, max_carry = state
        return jnp.logical_and(j >= 0, max_carry > SB_SKIP_BELOW_LOG2)

    def body(state):
        j, _ = state
        start = pl.multiple_of(j * blk, blk)
        carry = carry_ref[...]
        w, col_sum = weights(scores(start, blk), carry, False)
        acc_ref[...] += _dot_tn(v_ref[0, pl.ds(start, blk), :], w)
        carry_ref[...] = carry + col_sum
        return j - 1, jnp.max(carry_ref[...])

    lax.while_loop(cond, body, (qi - 2, jnp.max(carry_ref[...])))

    dim = lax.broadcasted_iota(jnp.int32, (LANES, blk), 0)
    out_t = jnp.where(dim < HEAD_DIM, acc_ref[:, :blk], acc_ref[:, blk:])
    o_ref[0] = out_t.T.astype(o_ref.dtype)


def _sb_suffix_matrix(blk):
    later = (jnp.arange(blk)[None, :] > jnp.arange(blk)[:, None])
    later = jnp.concatenate([later, later], axis=1)
    every = jnp.ones((SUM_ROWS, 2 * blk), dtype=bool)
    return jnp.concatenate([later, every], axis=0).astype(jnp.bfloat16)


def _sb_attention(proj3, mixed_cols):
    b, s, _ = proj3.shape
    blk = SB_QB
    n_pairs = mixed_cols // 2 // LANES
    return pl.pallas_call(
        functools.partial(_sb_kernel, blk=blk),
        grid=(b, n_pairs, s // blk),
        in_specs=[
            pl.BlockSpec((1, blk, LANES), lambda bi, hp, qi: (bi, qi, hp)),
            pl.BlockSpec((1, s, LANES), lambda bi, hp, qi: (bi, 0, n_pairs + hp)),
            pl.BlockSpec((1, s, LANES), lambda bi, hp, qi: (bi, 0, 2 * n_pairs + hp)),
            pl.BlockSpec((blk + SUM_ROWS, 2 * blk), lambda bi, hp, qi: (0, 0)),
        ],
        out_specs=pl.BlockSpec((1, blk, LANES), lambda bi, hp, qi: (bi, qi, hp)),
        out_shape=jax.ShapeDtypeStruct((b, s, n_pairs * LANES), jnp.bfloat16),
        scratch_shapes=[
            pltpu.VMEM((LANES, 2 * blk), jnp.float32),
            pltpu.VMEM((1, 2 * blk), jnp.float32),
        ],
        compiler_params=pltpu.CompilerParams(
            dimension_semantics=("parallel", "parallel", "arbitrary"),
            vmem_limit_bytes=VMEM_LIMIT),
        name="sb_attn",
    )(proj3, proj3, proj3, _sb_suffix_matrix(blk))


def _df_kernel(q_ref, k_ref, v_ref, lq1_ref, lk1_ref, lq2_ref, lk2_ref, g_ref, o_ref,
               acc_ref, m_ref, l_ref, s_ref, p_ref, alpha_ref, *, blk, lambda_init):
    qi = pl.program_id(2)
    q_halves = _split_halves(q_ref[0])
    key = lax.broadcasted_iota(jnp.int32, (blk, blk), 0)
    qry = lax.broadcasted_iota(jnp.int32, (blk, blk), 1)
    allowed = (key // CHUNK) <= (qry // CHUNK)

    acc_ref[...] = jnp.zeros_like(acc_ref)
    l_ref[...] = jnp.zeros_like(l_ref)
    m_ref[...] = jnp.full_like(m_ref, NEG_BIG)
    p_ref[...] = jnp.zeros_like(p_ref)
    alpha_ref[...] = jnp.ones_like(alpha_ref)

    def stage_values(j):
        start = pl.multiple_of(jnp.maximum(j, 0) * blk, blk)
        v = v_ref[0, pl.ds(start, blk), :]
        for h in range(2):
            acc_ref[h] = alpha_ref[h] * acc_ref[h] + _dot_tn(v, p_ref[h])

    def stage_softmax():
        for h in range(2):
            s = s_ref[h]
            m_old = m_ref[h]
            m_new = jnp.maximum(m_old, jnp.max(s, axis=0, keepdims=True))
            alpha = jnp.exp(m_old - m_new)
            p = jnp.exp(s - m_new)
            l_ref[h] = alpha * l_ref[h] + jnp.sum(p, axis=0, keepdims=True)
            p_ref[h] = p.astype(p_ref.dtype)
            alpha_ref[h] = alpha
            m_ref[h] = m_new

    def stage_scores(j, mask):
        start = pl.multiple_of(j * blk, blk)
        k = k_ref[0, pl.ds(start, blk), :]
        for h in range(2):
            s = _dot_nt(k, q_halves[h])
            if mask is not None:
                s = jnp.where(mask, s, NEG_BIG)
            s_ref[h] = s

    stage_scores(0, jnp.logical_or(allowed, qi > 0))

    def body(t, c):
        stage_values(t - 2)
        stage_softmax()
        stage_scores(t, None)
        return c

    lax.fori_loop(1, qi, body, 0)

    @pl.when(qi >= 1)
    def _():
        stage_values(qi - 2)
        stage_softmax()
        stage_scores(qi, allowed)

    stage_values(qi - 1)
    stage_softmax()
    stage_values(qi)

    lam = (jnp.exp(jnp.sum(lq1_ref[...] * lk1_ref[...]))
           - jnp.exp(jnp.sum(lq2_ref[...] * lk2_ref[...])) + lambda_init)
    out_t = acc_ref[0] / l_ref[0] - lam * (acc_ref[1] / l_ref[1])
    ms = jnp.mean(out_t * out_t, axis=0, keepdims=True)
    out_t = out_t * lax.rsqrt(ms + EPS) * g_ref[...] * (1.0 - lambda_init)
    o_ref[0] = out_t.T.astype(o_ref.dtype)


def _df_attention(proj3, lq1, lk1, lq2, lk2, g_col, mixed_cols, lambda_init):
    b, s, _ = proj3.shape
    blk = DF_QB
    n_heads = mixed_cols // 2 // LANES
    base = 3 * n_heads
    vec = pl.BlockSpec((1, HEAD_DIM), lambda bi, h, qi: (0, 0))
    return pl.pallas_call(
        functools.partial(_df_kernel, blk=blk, lambda_init=lambda_init),
        grid=(b, n_heads, s // blk),
        in_specs=[
            pl.BlockSpec((1, blk, LANES), lambda bi, h, qi: (bi, qi, base + h)),
            pl.BlockSpec((1, s, LANES), lambda bi, h, qi: (bi, 0, base + n_heads + h)),
            pl.BlockSpec((1, s, LANES), lambda bi, h, qi: (bi, 0, base + 2 * n_heads + h)),
            vec, vec, vec, vec,
            pl.BlockSpec((LANES, 1), lambda bi, h, qi: (0, 0)),
        ],
        out_specs=pl.BlockSpec((1, blk, LANES), lambda bi, h, qi: (bi, qi, h)),
        out_shape=jax.ShapeDtypeStruct((b, s, n_heads * LANES), jnp.bfloat16),
        scratch_shapes=[
            pltpu.VMEM((2, LANES, blk), jnp.float32),
            pltpu.VMEM((2, 1, blk), jnp.float32),
            pltpu.VMEM((2, 1, blk), jnp.float32),
            pltpu.VMEM((2, blk, blk), jnp.float32),
            pltpu.VMEM((2, blk, blk), jnp.bfloat16),
            pltpu.VMEM((2, 1, blk), jnp.float32),
        ],
        compiler_params=pltpu.CompilerParams(
            dimension_semantics=("parallel", "parallel", "arbitrary"),
            vmem_limit_bytes=VMEM_LIMIT),
        name="df_attn",
    )(proj3, proj3, proj3, lq1, lk1, lq2, lk2, g_col)


def _mlp_kernel(x_ref, sb_ref, df_ref, wo_ref, g_ref, w1_ref, w2_ref, gf_ref, o_ref, *, final):
    half = sb_ref.shape[1]
    x1 = (x_ref[...] + _dot(sb_ref[...], wo_ref[:half, :]) + _dot(df_ref[...], wo_ref[half:, :]))
    h = _rmsnorm(x1, g_ref[...]).astype(jnp.bfloat16)
    y = None
    for c in range(w1_ref.shape[1] // FF_CHUNK):
        u = _dot(h, w1_ref[:, c * FF_CHUNK:(c + 1) * FF_CHUNK])
        u = jnp.square(jnp.maximum(u, 0.0)).astype(jnp.bfloat16)
        t = _dot(u, w2_ref[c * FF_CHUNK:(c + 1) * FF_CHUNK, :])
        y = t if y is None else y + t
    out = x1 + y
    if final:
        out = _rmsnorm(out, gf_ref[...])
    o_ref[...] = out


def _mlp(x2d, sb2d, df2d, wo, g, w1, w2, gf, final):
    n, d = x2d.shape
    half = sb2d.shape[1]
    ff = w1.shape[1]
    tm = MLP_TM
    const = lambda i: (0, 0)
    resident = dict(pipeline_mode=pl.Buffered(1))
    return pl.pallas_call(
        functools.partial(_mlp_kernel, final=final),
        grid=(n // tm,),
        in_specs=[
            pl.BlockSpec((tm, d), lambda i: (i, 0)),
            pl.BlockSpec((tm, half), lambda i: (i, 0)),
            pl.BlockSpec((tm, half), lambda i: (i, 0)),
            pl.BlockSpec((2 * half, d), const, **resident),
            pl.BlockSpec((1, d), const),
            pl.BlockSpec((d, ff), const, **resident),
            pl.BlockSpec((ff, d), const, **resident),
            pl.BlockSpec((1, d), const),
        ],
        out_specs=pl.BlockSpec((tm, d), lambda i: (i, 0)),
        out_shape=jax.ShapeDtypeStruct((n, d), jnp.float32),
        compiler_params=pltpu.CompilerParams(
            dimension_semantics=("parallel",), vmem_limit_bytes=VMEM_LIMIT),
        name="mlp",
    )(x2d, sb2d, df2d, wo, g, w1, w2, gf)


def _rope_tables(seq):
    half = HEAD_DIM // 2
    inv = 1.0 / (ROPE_THETA ** (jnp.arange(0, HEAD_DIM, 2, dtype=jnp.float32) / HEAD_DIM))
    ang = jnp.arange(seq, dtype=jnp.float32)[:, None] * inv[None, :]
    ang = jnp.concatenate([ang, ang], axis=-1)
    cos = jnp.cos(ang)
    sin = jnp.sin(ang)
    lo = jnp.arange(HEAD_DIM) < half
    sin_lo = jnp.where(lo, -sin, 0.0)
    sin_hi = jnp.where(lo, 0.0, sin)
    rep = LANES // HEAD_DIM
    return tuple(jnp.tile(t, (1, rep)) for t in (cos, sin_lo, sin_hi))


def kernel(x, w_in, w_o, attn_norm, subln_norm, lam_q1, lam_k1, lam_q2, lam_k2, mlp_norm, w_ff1, w_ff2, final_norm):
    b, s, d = x.shape
    depth = w_in.shape[0]
    mixed_cols = w_o.shape[1]
    assert s % PROJ_TM == 0 and (b * s) % MLP_TM == 0
    assert s % SB_QB == 0 and s % DF_QB == 0 and DF_QB % CHUNK == 0
    assert w_in.shape[2] == 3 * mixed_cols and subln_norm.shape[1] == LANES

    cos, sin_lo, sin_hi = _rope_tables(s)
    bf = jnp.bfloat16
    x2d = x.reshape(b * s, d)
    for l in range(depth):
        lambda_init = 0.8 - 0.6 * math.exp(-0.3 * l)
        proj = _proj(x2d, attn_norm[l][None, :], w_in[l].astype(bf), cos, sin_lo, sin_hi, s)
        proj3 = proj.reshape(b, s, -1)
        sb = _sb_attention(proj3, mixed_cols)
        df = _df_attention(proj3, lam_q1[l][None, :], lam_k1[l][None, :], lam_q2[l][None, :],
                           lam_k2[l][None, :], subln_norm[l][:, None], mixed_cols, lambda_init)
        x2d = _mlp(x2d, sb.reshape(b * s, -1), df.reshape(b * s, -1), w_o[l].astype(bf),
                   mlp_norm[l][None, :], w_ff1[l].astype(bf), w_ff2[l].astype(bf),
                   final_norm[None, :], final=(l == depth - 1))
    return x2d.reshape(b, s, d)
```

```python
import functools
import math

import jax
import jax.numpy as jnp
from jax import lax
from jax.experimental import pallas as pl
from jax.experimental.pallas import tpu as pltpu

EPS = 1e-6
ROPE_THETA = 10000.0
HEAD_DIM = 64
LANES = 128
CHUNK = 64
VMEM_LIMIT = 56 * 1024 * 1024

PROJ_TM = 512
MLP_TM = 512
FF_CHUNK = 1024
SB_QB = 256
SB_HEADS = 4
DF_QB = 512
DF_HEADS = 1
LOG2E = 1.4426950408889634
SB_SKIP_BELOW_LOG2 = -130.0
SUM_ROWS = 16
NEG_BIG = -1e30


def _dot(a, b):
    return jnp.dot(a, b, preferred_element_type=jnp.float32)


def _dot_nt(a, b):
    return lax.dot_general(a, b, (((1,), (1,)), ((), ())),
                           preferred_element_type=jnp.float32)


def _dot_tn(a, b):
    return lax.dot_general(a, b, (((0,), (0,)), ((), ())),
                           preferred_element_type=jnp.float32)


def _rmsnorm(x, g):
    return x * lax.rsqrt(jnp.mean(x * x, axis=-1, keepdims=True) + EPS) * g


def _proj_kernel(x_ref, g_ref, w_ref, cos_ref, sin_lo_ref, sin_hi_ref, o_ref, *, width):
    h = _rmsnorm(x_ref[...], g_ref[...]).astype(jnp.bfloat16)
    scale = HEAD_DIM ** -0.5
    cos = cos_ref[...]
    sin_lo = sin_lo_ref[...]
    sin_hi = sin_hi_ref[...]
    for grp in range(6):
        p = _dot(h, w_ref[:, grp * width:(grp + 1) * width])
        rotary = grp in (3, 4)
        scaled = grp in (0, 3)
        for c in range(width // LANES):
            t = p[:, c * LANES:(c + 1) * LANES]
            if rotary:
                t = (t * cos + pltpu.roll(t, LANES - HEAD_DIM // 2, 1) * sin_lo
                     + pltpu.roll(t, HEAD_DIM // 2, 1) * sin_hi)
            if scaled:
                t = t * scale
            o_ref[:, grp * width + c * LANES: grp * width + (c + 1) * LANES] = t.astype(o_ref.dtype)


def _proj(x2d, g, w, cos, sin_lo, sin_hi, seq):
    n, d = x2d.shape
    pw = w.shape[1]
    tm = PROJ_TM
    n_pos_blocks = seq // tm
    tab_spec = pl.BlockSpec((tm, LANES), lambda i: (i % n_pos_blocks, 0))
    return pl.pallas_call(
        functools.partial(_proj_kernel, width=pw // 6),
        grid=(n // tm,),
        in_specs=[
            pl.BlockSpec((tm, d), lambda i: (i, 0)),
            pl.BlockSpec((1, d), lambda i: (0, 0)),
            pl.BlockSpec((d, pw), lambda i: (0, 0), pipeline_mode=pl.Buffered(1)),
            tab_spec, tab_spec, tab_spec,
        ],
        out_specs=pl.BlockSpec((tm, pw), lambda i: (i, 0)),
        out_shape=jax.ShapeDtypeStruct((n, pw), jnp.bfloat16),
        compiler_params=pltpu.CompilerParams(
            dimension_semantics=("parallel",), vmem_limit_bytes=VMEM_LIMIT),
        name="proj",
    )(x2d, g, w, cos, sin_lo, sin_hi)


def _head_copies(q):
    lane = lax.broadcasted_iota(jnp.int32, q.shape, 1)
    zero = jnp.zeros_like(q)
    return [jnp.where((lane >= h * HEAD_DIM) & (lane < (h + 1) * HEAD_DIM), q, zero)
            for h in range(q.shape[1] // HEAD_DIM)]


def _sb_kernel(q_ref, k_ref, v_ref, tri_ref, o_ref, acc_ref, carry_ref, *, blk):
    qi = pl.program_id(2)
    copies = _head_copies(q_ref[0])
    heads = len(copies)
    q2 = -jnp.concatenate(copies, axis=0)
    key = lax.broadcasted_iota(jnp.int32, (blk, heads * blk), 0)
    qry = lax.broadcasted_iota(jnp.int32, (blk, heads * blk), 1) & (blk - 1)
    strict = key < qry

    def scores(start, n_keys):
        k = k_ref[0, pl.ds(start, n_keys), :]
        return _dot_nt(k, q2) * LOG2E

    def weights(zn2, carry, masked):
        log_keep = jnp.minimum(zn2, 0.0) - jnp.log2(1.0 + jnp.exp2(-jnp.abs(zn2)))
        log_beta = log_keep - zn2
        if masked:
            log_keep = jnp.where(strict, log_keep, 0.0)
        hi = log_keep.astype(jnp.bfloat16)
        lo = (log_keep - hi.astype(jnp.float32)).astype(jnp.bfloat16)
        sums = _dot(tri_ref[...], jnp.concatenate([hi, lo], axis=0))
        arg = log_beta + sums[:blk]
        if carry is not None:
            arg = arg + carry
        w = jnp.exp2(arg)
        if masked:
            w = jnp.where(strict, w, 0.0)
        return w.astype(jnp.bfloat16), sums[blk:blk + 1]

    @pl.when(qi == 0)
    def _():
        w, col_sum = weights(scores(0, blk), None, True)
        acc_ref[...] = _dot_tn(v_ref[0, pl.ds(0, blk), :], w)
        carry_ref[...] = col_sum

    @pl.when(qi > 0)
    def _():
        start = pl.multiple_of((qi - 1) * blk, blk)
        zn2 = scores(start, 2 * blk)
        w_diag, sum_diag = weights(zn2[blk:], None, True)
        w_left, sum_left = weights(zn2[:blk], sum_diag, False)
        w = jnp.concatenate([w_left, w_diag], axis=0)
        acc_ref[...] = _dot_tn(v_ref[0, pl.ds(start, 2 * blk), :], w)
        carry_ref[...] = sum_diag + sum_left

    def cond(state):
        j, max_carry = state
        return jnp.logical_and(j >= 0, max_carry > SB_SKIP_BELOW_LOG2)

    def body(state):
        j, _ = state
        start = pl.multiple_of(j * blk, blk)
        carry = carry_ref[...]
        w, col_sum = weights(scores(start, blk), carry, False)
        acc_ref[...] += _dot_tn(v_ref[0, pl.ds(start, blk), :], w)
        carry_ref[...] = carry + col_sum
        return j - 1, jnp.max(carry_ref[...])

    lax.while_loop(cond, body, (qi - 2, jnp.max(carry_ref[...])))

    dim = lax.broadcasted_iota(jnp.int32, (heads * HEAD_DIM, blk), 0)
    out_t = acc_ref[:, :blk]
    for h in range(1, heads):
        out_t = jnp.where(dim >= h * HEAD_DIM, acc_ref[:, h * blk:(h + 1) * blk], out_t)
    o_ref[0] = out_t.T.astype(o_ref.dtype)


def _sb_suffix_matrix(blk):
    later = (jnp.arange(blk)[None, :] > jnp.arange(blk)[:, None])
    later = jnp.concatenate([later, later], axis=1)
    every = jnp.ones((SUM_ROWS, 2 * blk), dtype=bool)
    return jnp.concatenate([later, every], axis=0).astype(jnp.bfloat16)


def _sb_attention(proj3, mixed_cols):
    b, s, _ = proj3.shape
    blk = SB_QB
    cols = SB_HEADS * HEAD_DIM
    n_groups = mixed_cols // 2 // cols
    return pl.pallas_call(
        functools.partial(_sb_kernel, blk=blk),
        grid=(b, n_groups, s // blk),
        in_specs=[
            pl.BlockSpec((1, blk, cols), lambda bi, g, qi: (bi, qi, g)),
            pl.BlockSpec((1, s, cols), lambda bi, g, qi: (bi, 0, n_groups + g)),
            pl.BlockSpec((1, s, cols), lambda bi, g, qi: (bi, 0, 2 * n_groups + g)),
            pl.BlockSpec((blk + SUM_ROWS, 2 * blk), lambda bi, g, qi: (0, 0)),
        ],
        out_specs=pl.BlockSpec((1, blk, cols), lambda bi, g, qi: (bi, qi, g)),
        out_shape=jax.ShapeDtypeStruct((b, s, n_groups * cols), jnp.bfloat16),
        scratch_shapes=[
            pltpu.VMEM((cols, SB_HEADS * blk), jnp.float32),
            pltpu.VMEM((1, SB_HEADS * blk), jnp.float32),
        ],
        compiler_params=pltpu.CompilerParams(
            dimension_semantics=("parallel", "parallel", "arbitrary"),
            vmem_limit_bytes=VMEM_LIMIT),
        name="sb_attn",
    )(proj3, proj3, proj3, _sb_suffix_matrix(blk))


def _df_kernel(q_ref, k_ref, v_ref, lq1_ref, lk1_ref, lq2_ref, lk2_ref, g_ref, o_ref,
               acc_ref, m_ref, vt_ref, s_ref, p_ref, alpha_ref, *, blk, lambda_init):
    qi = pl.program_id(2)
    q_maps = _head_copies(q_ref[0])
    n_maps = len(q_maps)
    n_val = 2 * HEAD_DIM
    key = lax.broadcasted_iota(jnp.int32, (blk, blk), 0)
    qry = lax.broadcasted_iota(jnp.int32, (blk, blk), 1)
    allowed = (key // CHUNK) <= (qry // CHUNK)

    @pl.when(qi == 0)
    def _():
        for hd in range(n_maps // 2):
            for c in range(v_ref.shape[1] // blk):
                vt_ref[hd, :n_val, c * blk:(c + 1) * blk] = (
                    v_ref[0, c * blk:(c + 1) * blk, hd * n_val:(hd + 1) * n_val].T)
        vt_ref[:, n_val:, :] = jnp.ones(
            (vt_ref.shape[0], vt_ref.shape[1] - n_val, vt_ref.shape[2]), vt_ref.dtype)

    acc_ref[...] = jnp.zeros_like(acc_ref)
    m_ref[...] = jnp.full_like(m_ref, NEG_BIG)
    p_ref[...] = jnp.zeros_like(p_ref)
    alpha_ref[...] = jnp.ones_like(alpha_ref)

    def stage_values(j):
        start = pl.multiple_of(jnp.maximum(j, 0) * blk, blk)
        for h in range(n_maps):
            vt = vt_ref[h // 2, :, pl.ds(start, blk)]
            acc_ref[h] = alpha_ref[h] * acc_ref[h] + _dot(vt, p_ref[h])

    def stage_softmax():
        for h in range(n_maps):
            s = s_ref[h]
            m_old = m_ref[h]
            m_new = jnp.maximum(m_old, jnp.max(s, axis=0, keepdims=True))
            alpha = jnp.exp(m_old - m_new)
            p_ref[h] = jnp.exp(s - m_new).astype(p_ref.dtype)
            alpha_ref[h] = alpha
            m_ref[h] = m_new

    def stage_scores(j, mask):
        start = pl.multiple_of(j * blk, blk)
        k = k_ref[0, pl.ds(start, blk), :]
        for h in range(n_maps):
            s = _dot_nt(k, q_maps[h])
            if mask is not None:
                s = jnp.where(mask, s, NEG_BIG)
            s_ref[h] = s

    stage_scores(0, jnp.logical_or(allowed, qi > 0))

    def body(t, c):
        stage_values(t - 2)
        stage_softmax()
        stage_scores(t, None)
        return c

    lax.fori_loop(1, qi, body, 0)

    @pl.when(qi >= 1)
    def _():
        stage_values(qi - 2)
        stage_softmax()
        stage_scores(qi, allowed)

    stage_values(qi - 1)
    stage_softmax()
    stage_values(qi)

    lam = (jnp.exp(jnp.sum(lq1_ref[...] * lk1_ref[...]))
           - jnp.exp(jnp.sum(lq2_ref[...] * lk2_ref[...])) + lambda_init)
    for hd in range(n_maps // 2):
        a1, a2 = acc_ref[2 * hd], acc_ref[2 * hd + 1]
        out_t = a1[:n_val] / a1[n_val:n_val + 1] - lam * (a2[:n_val] / a2[n_val:n_val + 1])
        ms = jnp.mean(out_t * out_t, axis=0, keepdims=True)
        out_t = out_t * lax.rsqrt(ms + EPS) * g_ref[...] * (1.0 - lambda_init)
        o_ref[0, :, hd * n_val:(hd + 1) * n_val] = out_t.T.astype(o_ref.dtype)


def _df_attention(proj3, lq1, lk1, lq2, lk2, g_col, mixed_cols, lambda_init):
    b, s, _ = proj3.shape
    blk = DF_QB
    cols = DF_HEADS * 2 * HEAD_DIM
    n_groups = mixed_cols // 2 // cols
    base = 3 * n_groups
    n_maps = 2 * DF_HEADS
    vec = pl.BlockSpec((1, HEAD_DIM), lambda bi, g, qi: (0, 0))
    return pl.pallas_call(
        functools.partial(_df_kernel, blk=blk, lambda_init=lambda_init),
        grid=(b, n_groups, s // blk),
        in_specs=[
            pl.BlockSpec((1, blk, cols), lambda bi, g, qi: (bi, qi, base + g)),
            pl.BlockSpec((1, s, cols), lambda bi, g, qi: (bi, 0, base + n_groups + g)),
            pl.BlockSpec((1, s, cols), lambda bi, g, qi: (bi, 0, base + 2 * n_groups + g)),
            vec, vec, vec, vec,
            pl.BlockSpec((2 * HEAD_DIM, 1), lambda bi, g, qi: (0, 0)),
        ],
        out_specs=pl.BlockSpec((1, blk, cols), lambda bi, g, qi: (bi, qi, g)),
        out_shape=jax.ShapeDtypeStruct((b, s, n_groups * cols), jnp.bfloat16),
        scratch_shapes=[
            pltpu.VMEM((n_maps, 2 * HEAD_DIM + SUM_ROWS, blk), jnp.float32),
            pltpu.VMEM((n_maps, 1, blk), jnp.float32),
            pltpu.VMEM((DF_HEADS, 2 * HEAD_DIM + SUM_ROWS, s), jnp.bfloat16),
            pltpu.VMEM((n_maps, blk, blk), jnp.float32),
            pltpu.VMEM((n_maps, blk, blk), jnp.bfloat16),
            pltpu.VMEM((n_maps, 1, blk), jnp.float32),
        ],
        compiler_params=pltpu.CompilerParams(
            dimension_semantics=("parallel", "parallel", "arbitrary"),
            vmem_limit_bytes=VMEM_LIMIT),
        name="df_attn",
    )(proj3, proj3, proj3, lq1, lk1, lq2, lk2, g_col)


def _mlp_kernel(x_ref, sb_ref, df_ref, wo_ref, g_ref, w1_ref, w2_ref, gf_ref, o_ref, *, final):
    half = sb_ref.shape[1]
    x1 = (x_ref[...] + _dot(sb_ref[...], wo_ref[:half, :]) + _dot(df_ref[...], wo_ref[half:, :]))
    h = _rmsnorm(x1, g_ref[...]).astype(jnp.bfloat16)
    y = None
    for c in range(w1_ref.shape[1] // FF_CHUNK):
        u = _dot(h, w1_ref[:, c * FF_CHUNK:(c + 1) * FF_CHUNK])
        u = jnp.square(jnp.maximum(u, 0.0)).astype(jnp.bfloat16)
        t = _dot(u, w2_ref[c * FF_CHUNK:(c + 1) * FF_CHUNK, :])
        y = t if y is None else y + t
    out = x1 + y
    if final:
        out = _rmsnorm(out, gf_ref[...])
    o_ref[...] = out


def _mlp(x2d, sb2d, df2d, wo, g, w1, w2, gf, final):
    n, d = x2d.shape
    half = sb2d.shape[1]
    ff = w1.shape[1]
    tm = MLP_TM
    const = lambda i: (0, 0)
    resident = dict(pipeline_mode=pl.Buffered(1))
    return pl.pallas_call(
        functools.partial(_mlp_kernel, final=final),
        grid=(n // tm,),
        in_specs=[
            pl.BlockSpec((tm, d), lambda i: (i, 0)),
            pl.BlockSpec((tm, half), lambda i: (i, 0)),
            pl.BlockSpec((tm, half), lambda i: (i, 0)),
            pl.BlockSpec((2 * half, d), const, **resident),
            pl.BlockSpec((1, d), const),
            pl.BlockSpec((d, ff), const, **resident),
            pl.BlockSpec((ff, d), const, **resident),
            pl.BlockSpec((1, d), const),
        ],
        out_specs=pl.BlockSpec((tm, d), lambda i: (i, 0)),
        out_shape=jax.ShapeDtypeStruct((n, d), jnp.float32),
        compiler_params=pltpu.CompilerParams(
            dimension_semantics=("parallel",), vmem_limit_bytes=VMEM_LIMIT),
        name="mlp",
    )(x2d, sb2d, df2d, wo, g, w1, w2, gf)


def _rope_tables(seq):
    half = HEAD_DIM // 2
    inv = 1.0 / (ROPE_THETA ** (jnp.arange(0, HEAD_DIM, 2, dtype=jnp.float32) / HEAD_DIM))
    ang = jnp.arange(seq, dtype=jnp.float32)[:, None] * inv[None, :]
    ang = jnp.concatenate([ang, ang], axis=-1)
    cos = jnp.cos(ang)
    sin = jnp.sin(ang)
    lo = jnp.arange(HEAD_DIM) < half
    sin_lo = jnp.where(lo, -sin, 0.0)
    sin_hi = jnp.where(lo, 0.0, sin)
    rep = LANES // HEAD_DIM
    return tuple(jnp.tile(t, (1, rep)) for t in (cos, sin_lo, sin_hi))


def kernel(x, w_in, w_o, attn_norm, subln_norm, lam_q1, lam_k1, lam_q2, lam_k2, mlp_norm, w_ff1, w_ff2, final_norm):
    b, s, d = x.shape
    depth = w_in.shape[0]
    mixed_cols = w_o.shape[1]
    assert s % PROJ_TM == 0 and (b * s) % MLP_TM == 0
    assert s % SB_QB == 0 and s % DF_QB == 0 and DF_QB % CHUNK == 0
    assert w_in.shape[2] == 3 * mixed_cols and subln_norm.shape[1] == LANES

    cos, sin_lo, sin_hi = _rope_tables(s)
    bf = jnp.bfloat16
    x2d = x.reshape(b * s, d)
    for l in range(depth):
        lambda_init = 0.8 - 0.6 * math.exp(-0.3 * l)
        proj = _proj(x2d, attn_norm[l][None, :], w_in[l].astype(bf), cos, sin_lo, sin_hi, s)
        proj3 = proj.reshape(b, s, -1)
        sb = _sb_attention(proj3, mixed_cols)
        df = _df_attention(proj3, lam_q1[l][None, :], lam_k1[l][None, :], lam_q2[l][None, :],
                           lam_k2[l][None, :], subln_norm[l][:, None], mixed_cols, lambda_init)
        x2d = _mlp(x2d, sb.reshape(b * s, -1), df.reshape(b * s, -1), w_o[l].astype(bf),
                   mlp_norm[l][None, :], w_ff1[l].astype(bf), w_ff2[l].astype(bf),
                   final_norm[None, :], final=(l == depth - 1))
    return x2d.reshape(b, s, d)
```

```python
import functools
import math

import jax
import jax.numpy as jnp
from jax import lax
from jax.experimental import pallas as pl
from jax.experimental.pallas import tpu as pltpu

EPS = 1e-6
ROPE_THETA = 10000.0
HEAD_DIM = 64
LANES = 128
CHUNK = 64
VMEM_LIMIT = 56 * 1024 * 1024

PROJ_TM = 512
MLP_TM = 512
FF_CHUNK = 1024
SB_QB = 256
SB_HEADS = 4
DF_QB = 512
DF_HEADS = 1
LOG2E = 1.4426950408889634
SB_SKIP_BELOW_LOG2 = -130.0
SUM_ROWS = 16
NEG_BIG = -1e30


def _dot(a, b):
    return jnp.dot(a, b, preferred_element_type=jnp.float32)


def _dot_nt(a, b):
    return lax.dot_general(a, b, (((1,), (1,)), ((), ())),
                           preferred_element_type=jnp.float32)


def _dot_tn(a, b):
    return lax.dot_general(a, b, (((0,), (0,)), ((), ())),
                           preferred_element_type=jnp.float32)


def _rmsnorm(x, g):
    return x * lax.rsqrt(jnp.mean(x * x, axis=-1, keepdims=True) + EPS) * g


def _proj_kernel(x_ref, g_ref, w_ref, cos_ref, sin_lo_ref, sin_hi_ref, o_ref, *, width):
    h = _rmsnorm(x_ref[...], g_ref[...]).astype(jnp.bfloat16)
    scale = HEAD_DIM ** -0.5
    cos = cos_ref[...]
    sin_lo = sin_lo_ref[...]
    sin_hi = sin_hi_ref[...]
    for grp in range(6):
        p = _dot(h, w_ref[:, grp * width:(grp + 1) * width])
        rotary = grp in (3, 4)
        scaled = grp in (0, 3)
        for c in range(width // LANES):
            t = p[:, c * LANES:(c + 1) * LANES]
            if rotary:
                t = (t * cos + pltpu.roll(t, LANES - HEAD_DIM // 2, 1) * sin_lo
                     + pltpu.roll(t, HEAD_DIM // 2, 1) * sin_hi)
            if scaled:
                t = t * scale
            o_ref[:, grp * width + c * LANES: grp * width + (c + 1) * LANES] = t.astype(o_ref.dtype)


def _proj(x2d, g, w, cos, sin_lo, sin_hi, seq):
    n, d = x2d.shape
    pw = w.shape[1]
    tm = PROJ_TM
    n_pos_blocks = seq // tm
    tab_spec = pl.BlockSpec((tm, LANES), lambda i: (i % n_pos_blocks, 0))
    return pl.pallas_call(
        functools.partial(_proj_kernel, width=pw // 6),
        grid=(n // tm,),
        in_specs=[
            pl.BlockSpec((tm, d), lambda i: (i, 0)),
            pl.BlockSpec((1, d), lambda i: (0, 0)),
            pl.BlockSpec((d, pw), lambda i: (0, 0), pipeline_mode=pl.Buffered(1)),
            tab_spec, tab_spec, tab_spec,
        ],
        out_specs=pl.BlockSpec((tm, pw), lambda i: (i, 0)),
        out_shape=jax.ShapeDtypeStruct((n, pw), jnp.bfloat16),
        compiler_params=pltpu.CompilerParams(
            dimension_semantics=("parallel",), vmem_limit_bytes=VMEM_LIMIT),
        name="proj",
    )(x2d, g, w, cos, sin_lo, sin_hi)


def _head_copies(q):
    lane = lax.broadcasted_iota(jnp.int32, q.shape, 1)
    zero = jnp.zeros_like(q)
    return [jnp.where((lane >= h * HEAD_DIM) & (lane < (h + 1) * HEAD_DIM), q, zero)
            for h in range(q.shape[1] // HEAD_DIM)]


def _sb_kernel(q_ref, k_ref, v_ref, tri_ref, o_ref, acc_ref, carry_ref, *, blk):
    qi = pl.program_id(2)
    copies = _head_copies(q_ref[0])
    heads = len(copies)
    q2 = -jnp.concatenate(copies, axis=0)
    key = lax.broadcasted_iota(jnp.int32, (blk, heads * blk), 0)
    qry = lax.broadcasted_iota(jnp.int32, (blk, heads * blk), 1) & (blk - 1)
    strict = key < qry

    def scores(start, n_keys):
        k = k_ref[0, pl.ds(start, n_keys), :]
        return _dot_nt(k, q2) * LOG2E

    def weights(zn2, carry, masked):
        log_keep = jnp.minimum(zn2, 0.0) - jnp.log2(1.0 + jnp.exp2(-jnp.abs(zn2)))
        log_beta = log_keep - zn2
        if masked:
            log_keep = jnp.where(strict, log_keep, 0.0)
        hi = log_keep.astype(jnp.bfloat16)
        lo = (log_keep - hi.astype(jnp.float32)).astype(jnp.bfloat16)
        sums = _dot(tri_ref[...], jnp.concatenate([hi, lo], axis=0))
        arg = log_beta + sums[:blk]
        if carry is not None:
            arg = arg + carry
        w = jnp.exp2(arg)
        if masked:
            w = jnp.where(strict, w, 0.0)
        return w.astype(jnp.bfloat16), sums[blk:blk + 1]

    @pl.when(qi == 0)
    def _():
        w, col_sum = weights(scores(0, blk), None, True)
        acc_ref[...] = _dot_tn(v_ref[0, pl.ds(0, blk), :], w)
        carry_ref[...] = col_sum

    @pl.when(qi > 0)
    def _():
        start = pl.multiple_of((qi - 1) * blk, blk)
        zn2 = scores(start, 2 * blk)
        w_diag, sum_diag = weights(zn2[blk:], None, True)
        w_left, sum_left = weights(zn2[:blk], sum_diag, False)
        w = jnp.concatenate([w_left, w_diag], axis=0)
        acc_ref[...] = _dot_tn(v_ref[0, pl.ds(start, 2 * blk), :], w)
        carry_ref[...] = sum_diag + sum_left

    def cond(state):
        j, max_carry = state
        return jnp.logical_and(j >= 0, max_carry > SB_SKIP_BELOW_LOG2)

    def body(state):
        j, _ = state
        start = pl.multiple_of(j * blk, blk)
        carry = carry_ref[...]
        w, col_sum = weights(scores(start, blk), carry, False)
        acc_ref[...] += _dot_tn(v_ref[0, pl.ds(start, blk), :], w)
        carry_ref[...] = carry + col_sum
        return j - 1, jnp.max(carry_ref[...])

    lax.while_loop(cond, body, (qi - 2, jnp.max(carry_ref[...])))

    dim = lax.broadcasted_iota(jnp.int32, (heads * HEAD_DIM, blk), 0)
    out_t = acc_ref[:, :blk]
    for h in range(1, heads):
        out_t = jnp.where(dim >= h * HEAD_DIM, acc_ref[:, h * blk:(h + 1) * blk], out_t)
    o_ref[0] = out_t.T.astype(o_ref.dtype)


def _sb_suffix_matrix(blk):
    later = (jnp.arange(blk)[None, :] > jnp.arange(blk)[:, None])
    later = jnp.concatenate([later, later], axis=1)
    every = jnp.ones((SUM_ROWS, 2 * blk), dtype=bool)
    return jnp.concatenate([later, every], axis=0).astype(jnp.bfloat16)


def _sb_attention(proj3, mixed_cols):
    b, s, _ = proj3.shape
    blk = SB_QB
    cols = SB_HEADS * HEAD_DIM
    n_groups = mixed_cols // 2 // cols
    return pl.pallas_call(
        functools.partial(_sb_kernel, blk=blk),
        grid=(b, n_groups, s // blk),
        in_specs=[
            pl.BlockSpec((1, blk, cols), lambda bi, g, qi: (bi, qi, g)),
            pl.BlockSpec((1, s, cols), lambda bi, g, qi: (bi, 0, n_groups + g)),
            pl.BlockSpec((1, s, cols), lambda bi, g, qi: (bi, 0, 2 * n_groups + g)),
            pl.BlockSpec((blk + SUM_ROWS, 2 * blk), lambda bi, g, qi: (0, 0)),
        ],
        out_specs=pl.BlockSpec((1, blk, cols), lambda bi, g, qi: (bi, qi, g)),
        out_shape=jax.ShapeDtypeStruct((b, s, n_groups * cols), jnp.bfloat16),
        scratch_shapes=[
            pltpu.VMEM((cols, SB_HEADS * blk), jnp.float32),
            pltpu.VMEM((1, SB_HEADS * blk), jnp.float32),
        ],
        compiler_params=pltpu.CompilerParams(
            dimension_semantics=("parallel", "parallel", "arbitrary"),
            vmem_limit_bytes=VMEM_LIMIT),
        name="sb_attn",
    )(proj3, proj3, proj3, _sb_suffix_matrix(blk))


def _df_kernel(q_ref, k_ref, v_ref, bias_ref, lq1_ref, lk1_ref, lq2_ref, lk2_ref, g_ref, o_ref,
               acc_ref, m_ref, vt_ref, s_ref, p_ref, alpha_ref, cmax_ref, *, blk, lambda_init):
    qi = pl.program_id(2)
    q_maps = _head_copies(q_ref[0])
    n_maps = len(q_maps)
    n_val = 2 * HEAD_DIM

    @pl.when(qi == 0)
    def _():
        for hd in range(n_maps // 2):
            for c in range(v_ref.shape[1] // blk):
                vt_ref[hd, :n_val, c * blk:(c + 1) * blk] = (
                    v_ref[0, c * blk:(c + 1) * blk, hd * n_val:(hd + 1) * n_val].T)
        vt_ref[:, n_val:, :] = jnp.ones(
            (vt_ref.shape[0], vt_ref.shape[1] - n_val, vt_ref.shape[2]), vt_ref.dtype)

    acc_ref[...] = jnp.zeros_like(acc_ref)
    m_ref[...] = jnp.full_like(m_ref, NEG_BIG)
    p_ref[...] = jnp.zeros_like(p_ref)
    alpha_ref[...] = jnp.ones_like(alpha_ref)

    def stage_values(j):
        start = pl.multiple_of(jnp.clip(j, 0, qi) * blk, blk)
        for h in range(n_maps):
            vt = vt_ref[h // 2, :, pl.ds(start, blk)]
            acc_ref[h] = alpha_ref[h] * acc_ref[h] + _dot(vt, p_ref[h])

    def stage_softmax():
        for h in range(n_maps):
            s = s_ref[h]
            m_old = m_ref[h]
            m_new = jnp.maximum(m_old, cmax_ref[h])
            alpha = jnp.exp(m_old - m_new)
            p_ref[h] = jnp.exp(s - m_new).astype(p_ref.dtype)
            alpha_ref[h] = alpha
            m_ref[h] = m_new

    def stage_scores(j, diagonal):
        start = pl.multiple_of(j * blk, blk)
        k = k_ref[0, pl.ds(start, blk), :]
        for h in range(n_maps):
            s = _dot_nt(k, q_maps[h])
            if diagonal:
                s = s + bias_ref[...]
            s_ref[h] = s
            cmax_ref[h] = jnp.max(s, axis=0, keepdims=True)

    stage_scores(qi, True)

    def body(t, c):
        stage_values(qi - t + 2)
        stage_softmax()
        stage_scores(qi - t, False)
        return c

    lax.fori_loop(1, qi + 1, body, 0)

    stage_values(1)
    stage_softmax()
    stage_values(0)

    lam = (jnp.exp(jnp.sum(lq1_ref[...] * lk1_ref[...]))
           - jnp.exp(jnp.sum(lq2_ref[...] * lk2_ref[...])) + lambda_init)
    for hd in range(n_maps // 2):
        a1, a2 = acc_ref[2 * hd], acc_ref[2 * hd + 1]
        out_t = a1[:n_val] / a1[n_val:n_val + 1] - lam * (a2[:n_val] / a2[n_val:n_val + 1])
        ms = jnp.mean(out_t * out_t, axis=0, keepdims=True)
        out_t = out_t * lax.rsqrt(ms + EPS) * g_ref[...] * (1.0 - lambda_init)
        o_ref[0, :, hd * n_val:(hd + 1) * n_val] = out_t.T.astype(o_ref.dtype)


def _df_diagonal_bias(blk):
    chunk = jnp.arange(blk) // CHUNK
    return jnp.where(chunk[:, None] <= chunk[None, :], 0.0, NEG_BIG).astype(jnp.float32)


def _df_attention(proj3, lq1, lk1, lq2, lk2, g_col, mixed_cols, lambda_init):
    b, s, _ = proj3.shape
    blk = DF_QB
    cols = DF_HEADS * 2 * HEAD_DIM
    n_groups = mixed_cols // 2 // cols
    base = 3 * n_groups
    n_maps = 2 * DF_HEADS
    vec = pl.BlockSpec((1, HEAD_DIM), lambda bi, g, qi: (0, 0))
    return pl.pallas_call(
        functools.partial(_df_kernel, blk=blk, lambda_init=lambda_init),
        grid=(b, n_groups, s // blk),
        in_specs=[
            pl.BlockSpec((1, blk, cols), lambda bi, g, qi: (bi, qi, base + g)),
            pl.BlockSpec((1, s, cols), lambda bi, g, qi: (bi, 0, base + n_groups + g)),
            pl.BlockSpec((1, s, cols), lambda bi, g, qi: (bi, 0, base + 2 * n_groups + g)),
            pl.BlockSpec((blk, blk), lambda bi, g, qi: (0, 0)),
            vec, vec, vec, vec,
            pl.BlockSpec((2 * HEAD_DIM, 1), lambda bi, g, qi: (0, 0)),
        ],
        out_specs=pl.BlockSpec((1, blk, cols), lambda bi, g, qi: (bi, qi, g)),
        out_shape=jax.ShapeDtypeStruct((b, s, n_groups * cols), jnp.bfloat16),
        scratch_shapes=[
            pltpu.VMEM((n_maps, 2 * HEAD_DIM + SUM_ROWS, blk), jnp.float32),
            pltpu.VMEM((n_maps, 1, blk), jnp.float32),
            pltpu.VMEM((DF_HEADS, 2 * HEAD_DIM + SUM_ROWS, s), jnp.bfloat16),
            pltpu.VMEM((n_maps, blk, blk), jnp.float32),
            pltpu.VMEM((n_maps, blk, blk), jnp.bfloat16),
            pltpu.VMEM((n_maps, 1, blk), jnp.float32),
            pltpu.VMEM((n_maps, 1, blk), jnp.float32),
        ],
        compiler_params=pltpu.CompilerParams(
            dimension_semantics=("parallel", "parallel", "arbitrary"),
            vmem_limit_bytes=VMEM_LIMIT),
        name="df_attn",
    )(proj3, proj3, proj3, _df_diagonal_bias(blk), lq1, lk1, lq2, lk2, g_col)


def _mlp_kernel(x_ref, sb_ref, df_ref, wo_ref, g_ref, w1_ref, w2_ref, gf_ref, o_ref, *, final):
    half = sb_ref.shape[1]
    x1 = (x_ref[...] + _dot(sb_ref[...], wo_ref[:half, :]) + _dot(df_ref[...], wo_ref[half:, :]))
    h = _rmsnorm(x1, g_ref[...]).astype(jnp.bfloat16)
    y = None
    for c in range(w1_ref.shape[1] // FF_CHUNK):
        u = _dot(h, w1_ref[:, c * FF_CHUNK:(c + 1) * FF_CHUNK])
        u = jnp.square(jnp.maximum(u, 0.0)).astype(jnp.bfloat16)
        t = _dot(u, w2_ref[c * FF_CHUNK:(c + 1) * FF_CHUNK, :])
        y = t if y is None else y + t
    out = x1 + y
    if final:
        out = _rmsnorm(out, gf_ref[...])
    o_ref[...] = out


def _mlp(x2d, sb2d, df2d, wo, g, w1, w2, gf, final):
    n, d = x2d.shape
    half = sb2d.shape[1]
    ff = w1.shape[1]
    tm = MLP_TM
    const = lambda i: (0, 0)
    resident = dict(pipeline_mode=pl.Buffered(1))
    return pl.pallas_call(
        functools.partial(_mlp_kernel, final=final),
        grid=(n // tm,),
        in_specs=[
            pl.BlockSpec((tm, d), lambda i: (i, 0)),
            pl.BlockSpec((tm, half), lambda i: (i, 0)),
            pl.BlockSpec((tm, half), lambda i: (i, 0)),
            pl.BlockSpec((2 * half, d), const, **resident),
            pl.BlockSpec((1, d), const),
            pl.BlockSpec((d, ff), const, **resident),
            pl.BlockSpec((ff, d), const, **resident),
            pl.BlockSpec((1, d), const),
        ],
        out_specs=pl.BlockSpec((tm, d), lambda i: (i, 0)),
        out_shape=jax.ShapeDtypeStruct((n, d), jnp.float32),
        compiler_params=pltpu.CompilerParams(
            dimension_semantics=("parallel",), vmem_limit_bytes=VMEM_LIMIT),
        name="mlp",
    )(x2d, sb2d, df2d, wo, g, w1, w2, gf)


def _rope_tables(seq):
    half = HEAD_DIM // 2
    inv = 1.0 / (ROPE_THETA ** (jnp.arange(0, HEAD_DIM, 2, dtype=jnp.float32) / HEAD_DIM))
    ang = jnp.arange(seq, dtype=jnp.float32)[:, None] * inv[None, :]
    ang = jnp.concatenate([ang, ang], axis=-1)
    cos = jnp.cos(ang)
    sin = jnp.sin(ang)
    lo = jnp.arange(HEAD_DIM) < half
    sin_lo = jnp.where(lo, -sin, 0.0)
    sin_hi = jnp.where(lo, 0.0, sin)
    rep = LANES // HEAD_DIM
    return tuple(jnp.tile(t, (1, rep)) for t in (cos, sin_lo, sin_hi))


def kernel(x, w_in, w_o, attn_norm, subln_norm, lam_q1, lam_k1, lam_q2, lam_k2, mlp_norm, w_ff1, w_ff2, final_norm):
    b, s, d = x.shape
    depth = w_in.shape[0]
    mixed_cols = w_o.shape[1]
    assert s % PROJ_TM == 0 and (b * s) % MLP_TM == 0
    assert s % SB_QB == 0 and s % DF_QB == 0 and DF_QB % CHUNK == 0
    assert w_in.shape[2] == 3 * mixed_cols and subln_norm.shape[1] == LANES

    cos, sin_lo, sin_hi = _rope_tables(s)
    bf = jnp.bfloat16
    x2d = x.reshape(b * s, d)
    for l in range(depth):
        lambda_init = 0.8 - 0.6 * math.exp(-0.3 * l)
        proj = _proj(x2d, attn_norm[l][None, :], w_in[l].astype(bf), cos, sin_lo, sin_hi, s)
        proj3 = proj.reshape(b, s, -1)
        sb = _sb_attention(proj3, mixed_cols)
        df = _df_attention(proj3, lam_q1[l][None, :], lam_k1[l][None, :], lam_q2[l][None, :],
                           lam_k2[l][None, :], subln_norm[l][:, None], mixed_cols, lambda_init)
        x2d = _mlp(x2d, sb.reshape(b * s, -1), df.reshape(b * s, -1), w_o[l].astype(bf),
                   mlp_norm[l][None, :], w_ff1[l].astype(bf), w_ff2[l].astype(bf),
                   final_norm[None, :], final=(l == depth - 1))
    return x2d.reshape(b, s, d)
```

```python
import functools
import math

import jax
import jax.numpy as jnp
from jax import lax
from jax.experimental import pallas as pl
from jax.experimental.pallas import tpu as pltpu

EPS = 1e-6
ROPE_THETA = 10000.0
HEAD_DIM = 64
LANES = 128
CHUNK = 64
VMEM_LIMIT = 56 * 1024 * 1024

PROJ_TM = 512
MLP_TM = 512
FF_CHUNK = 1024
SB_QB = 256
SB_HEADS = 4
DF_QB = 512
DF_HEADS = 1
LOG2E = 1.4426950408889634
SB_SKIP_BELOW_LOG2 = -130.0
SUM_ROWS = 16
NEG_BIG = -1e30


def _dot(a, b):
    return jnp.dot(a, b, preferred_element_type=jnp.float32)


def _dot_nt(a, b):
    return lax.dot_general(a, b, (((1,), (1,)), ((), ())),
                           preferred_element_type=jnp.float32)


def _dot_tn(a, b):
    return lax.dot_general(a, b, (((0,), (0,)), ((), ())),
                           preferred_element_type=jnp.float32)


def _rmsnorm(x, g):
    return x * lax.rsqrt(jnp.mean(x * x, axis=-1, keepdims=True) + EPS) * g


def _proj_kernel(x_ref, g_ref, w_ref, cos_ref, sin_lo_ref, sin_hi_ref, o_ref, *, width):
    h = _rmsnorm(x_ref[...], g_ref[...]).astype(jnp.bfloat16)
    scale = HEAD_DIM ** -0.5
    group_scale = {0: scale * LOG2E, 3: scale}
    cos = cos_ref[...]
    sin_lo = sin_lo_ref[...]
    sin_hi = sin_hi_ref[...]
    for grp in range(6):
        p = _dot(h, w_ref[:, grp * width:(grp + 1) * width])
        rotary = grp in (3, 4)
        for c in range(width // LANES):
            t = p[:, c * LANES:(c + 1) * LANES]
            if rotary:
                t = (t * cos + pltpu.roll(t, LANES - HEAD_DIM // 2, 1) * sin_lo
                     + pltpu.roll(t, HEAD_DIM // 2, 1) * sin_hi)
            if grp in group_scale:
                t = t * group_scale[grp]
            o_ref[:, grp * width + c * LANES: grp * width + (c + 1) * LANES] = t.astype(o_ref.dtype)


def _proj(x2d, g, w, cos, sin_lo, sin_hi, seq):
    n, d = x2d.shape
    pw = w.shape[1]
    tm = PROJ_TM
    n_pos_blocks = seq // tm
    tab_spec = pl.BlockSpec((tm, LANES), lambda i: (i % n_pos_blocks, 0))
    return pl.pallas_call(
        functools.partial(_proj_kernel, width=pw // 6),
        grid=(n // tm,),
        in_specs=[
            pl.BlockSpec((tm, d), lambda i: (i, 0)),
            pl.BlockSpec((1, d), lambda i: (0, 0)),
            pl.BlockSpec((d, pw), lambda i: (0, 0), pipeline_mode=pl.Buffered(1)),
            tab_spec, tab_spec, tab_spec,
        ],
        out_specs=pl.BlockSpec((tm, pw), lambda i: (i, 0)),
        out_shape=jax.ShapeDtypeStruct((n, pw), jnp.bfloat16),
        compiler_params=pltpu.CompilerParams(
            dimension_semantics=("parallel",), vmem_limit_bytes=VMEM_LIMIT),
        name="proj",
    )(x2d, g, w, cos, sin_lo, sin_hi)


def _head_copies(q):
    lane = lax.broadcasted_iota(jnp.int32, q.shape, 1)
    zero = jnp.zeros_like(q)
    return [jnp.where((lane >= h * HEAD_DIM) & (lane < (h + 1) * HEAD_DIM), q, zero)
            for h in range(q.shape[1] // HEAD_DIM)]


def _sb_kernel(q_ref, k_ref, v_ref, tri_ref, o_ref,
               lb_ref, cat_ref, w_ref, carry_ref, extra_ref, *, blk):
    n_q = q_ref.shape[1] // blk
    heads = q_ref.shape[2] // HEAD_DIM
    key = lax.broadcasted_iota(jnp.int32, (blk, blk), 0)
    qry = lax.broadcasted_iota(jnp.int32, (blk, blk), 1)
    strict = key < qry

    def rows(ref, block):
        return ref[0, pl.ds(pl.multiple_of(block * blk, blk), blk), :]

    def window(ref, block):
        return jnp.concatenate([rows(ref, jnp.maximum(block - 1, 0)), rows(ref, block)], axis=0)

    def log_terms(zn2):
        neg_abs = lax.bitcast_convert_type(
            lax.bitcast_convert_type(zn2, jnp.uint32) | jnp.uint32(0x80000000), jnp.float32)
        log_keep = jnp.minimum(zn2, 0.0) - jnp.log2(1.0 + jnp.exp2(neg_abs))
        return log_keep, log_keep - zn2

    def split(log_keep):
        hi = lax.bitcast_convert_type(
            lax.bitcast_convert_type(log_keep, jnp.uint32) & jnp.uint32(0xFFFF0000), jnp.float32)
        lo = (log_keep - hi).astype(jnp.bfloat16)
        hi = hi.astype(jnp.bfloat16)
        return jnp.concatenate([hi, lo], axis=0)

    def suffix_sums(cat):
        return _dot(tri_ref[...], cat)

    lb_ref[...] = jnp.zeros_like(lb_ref)
    cat_ref[...] = jnp.zeros_like(cat_ref)
    w_ref[...] = jnp.zeros_like(w_ref)
    carry_ref[...] = jnp.zeros_like(carry_ref)
    extra_ref[...] = jnp.zeros_like(extra_ref)

    def step(t, p):
        cur = jnp.minimum(t, n_q - 1)
        fin = jnp.clip(t - 2, 0, n_q - 1)
        far = jnp.clip(t - 1, 0, n_q - 1)
        k_win = window(k_ref, cur)
        v_win = window(v_ref, fin)
        neg_q = [-c for c in _head_copies(rows(q_ref, cur))]
        no_left = jnp.where(t == 1, NEG_BIG, 0.0)
        outs = []
        for h in range(heads):
            log_keep, log_beta = log_terms(_dot_nt(k_win, neg_q[h]))
            lb_ref[p, h] = log_beta
            cat_ref[p, h, 0] = split(log_keep[:blk])
            cat_ref[p, h, 1] = split(jnp.where(strict, log_keep[blk:], 0.0))

            sums_diag = suffix_sums(cat_ref[1 - p, h, 1])
            w_diag = jnp.where(strict, jnp.exp2(lb_ref[1 - p, h, blk:] + sums_diag[:blk]), 0.0)
            sum_diag = sums_diag[blk:blk + 1]
            sums_left = suffix_sums(cat_ref[1 - p, h, 0])
            w_left = jnp.exp2(lb_ref[1 - p, h, :blk] + sums_left[:blk] + (sum_diag + no_left))
            w_ref[1 - p, h, :blk] = w_left.astype(w_ref.dtype)
            w_ref[1 - p, h, blk:] = w_diag.astype(w_ref.dtype)
            carry_ref[h] = sum_diag + sums_left[blk:blk + 1]
            extra_ref[1 - p, h] = jnp.zeros(extra_ref.shape[2:], extra_ref.dtype)

            v_h = v_win[:, h * HEAD_DIM:(h + 1) * HEAD_DIM]
            outs.append(_dot_tn(v_h, w_ref[p, h]) + extra_ref[p, h])
        out_t = jnp.concatenate(outs, axis=0)
        o_ref[0, pl.ds(pl.multiple_of(fin * blk, blk), blk), :] = out_t.T.astype(o_ref.dtype)

        def cond(state):
            j, max_carry = state
            return jnp.logical_and(j >= 0, max_carry > SB_SKIP_BELOW_LOG2)

        def body(state):
            j, _ = state
            k_far, v_far = rows(k_ref, j), rows(v_ref, j)
            neg_q_far = [-c for c in _head_copies(rows(q_ref, far))]
            for h in range(heads):
                carry = carry_ref[h]
                log_keep, log_beta = log_terms(_dot_nt(k_far, neg_q_far[h]))
                sums = suffix_sums(split(log_keep))
                w = jnp.exp2(log_beta + sums[:blk] + carry)
                extra_ref[1 - p, h] += _dot_tn(v_far[:, h * HEAD_DIM:(h + 1) * HEAD_DIM], w.astype(jnp.bfloat16))
                carry_ref[h] = carry + sums[blk:blk + 1]
            return j - 1, jnp.max(carry_ref[...])

        lax.while_loop(cond, body, (far - 2, jnp.max(carry_ref[...])))

    def two_steps(u, c):
        step(2 * u, 0)
        step(2 * u + 1, 1)
        return c

    lax.fori_loop(0, (n_q + 2) // 2, two_steps, 0)


def _sb_suffix_matrix(blk):
    later = (jnp.arange(blk)[None, :] > jnp.arange(blk)[:, None])
    later = jnp.concatenate([later, later], axis=1)
    every = jnp.ones((SUM_ROWS, 2 * blk), dtype=bool)
    return jnp.concatenate([later, every], axis=0).astype(jnp.bfloat16)


def _sb_attention(proj3, mixed_cols):
    b, s, _ = proj3.shape
    blk = SB_QB
    cols = SB_HEADS * HEAD_DIM
    n_groups = mixed_cols // 2 // cols
    return pl.pallas_call(
        functools.partial(_sb_kernel, blk=blk),
        grid=(b, n_groups),
        in_specs=[
            pl.BlockSpec((1, s, cols), lambda bi, g: (bi, 0, g)),
            pl.BlockSpec((1, s, cols), lambda bi, g: (bi, 0, n_groups + g)),
            pl.BlockSpec((1, s, cols), lambda bi, g: (bi, 0, 2 * n_groups + g)),
            pl.BlockSpec((blk + SUM_ROWS, 2 * blk), lambda bi, g: (0, 0)),
        ],
        out_specs=pl.BlockSpec((1, s, cols), lambda bi, g: (bi, 0, g)),
        out_shape=jax.ShapeDtypeStruct((b, s, n_groups * cols), jnp.bfloat16),
        scratch_shapes=[
            pltpu.VMEM((2, SB_HEADS, 2 * blk, blk), jnp.float32),
            pltpu.VMEM((2, SB_HEADS, 2, 2 * blk, blk), jnp.bfloat16),
            pltpu.VMEM((2, SB_HEADS, 2 * blk, blk), jnp.bfloat16),
            pltpu.VMEM((SB_HEADS, 1, blk), jnp.float32),
            pltpu.VMEM((2, SB_HEADS, HEAD_DIM, blk), jnp.float32),
        ],
        compiler_params=pltpu.CompilerParams(
            dimension_semantics=("parallel", "parallel"),
            vmem_limit_bytes=VMEM_LIMIT),
        name="sb_attn",
    )(proj3, proj3, proj3, _sb_suffix_matrix(blk))


def _df_kernel(q_ref, k_ref, v_ref, bias_ref, lq1_ref, lk1_ref, lq2_ref, lk2_ref, g_ref, o_ref,
               acc_ref, m_ref, vt_ref, s_ref, p_ref, alpha_ref, cmax_ref, *, blk, lambda_init):
    qi = pl.program_id(2)
    q_maps = _head_copies(q_ref[0])
    n_maps = len(q_maps)
    n_val = 2 * HEAD_DIM

    @pl.when(qi == 0)
    def _():
        for hd in range(n_maps // 2):
            for c in range(v_ref.shape[1] // blk):
                vt_ref[hd, :n_val, c * blk:(c + 1) * blk] = (
                    v_ref[0, c * blk:(c + 1) * blk, hd * n_val:(hd + 1) * n_val].T)
        vt_ref[:, n_val:, :] = jnp.ones(
            (vt_ref.shape[0], vt_ref.shape[1] - n_val, vt_ref.shape[2]), vt_ref.dtype)

    acc_ref[...] = jnp.zeros_like(acc_ref)
    m_ref[...] = jnp.full_like(m_ref, NEG_BIG)
    p_ref[...] = jnp.zeros_like(p_ref)
    alpha_ref[...] = jnp.ones_like(alpha_ref)

    def stage_values(j):
        start = pl.multiple_of(jnp.clip(j, 0, qi) * blk, blk)
        for h in range(n_maps):
            vt = vt_ref[h // 2, :, pl.ds(start, blk)]
            acc_ref[h] = alpha_ref[h] * acc_ref[h] + _dot(vt, p_ref[h])

    def stage_softmax():
        for h in range(n_maps):
            s = s_ref[h]
            m_old = m_ref[h]
            m_new = jnp.maximum(m_old, cmax_ref[h])
            alpha = jnp.exp(m_old - m_new)
            p_ref[h] = jnp.exp(s - m_new).astype(p_ref.dtype)
            alpha_ref[h] = alpha
            m_ref[h] = m_new

    def stage_scores(j, diagonal):
        start = pl.multiple_of(j * blk, blk)
        k = k_ref[0, pl.ds(start, blk), :]
        for h in range(n_maps):
            s = _dot_nt(k, q_maps[h])
            if diagonal:
                s = s + bias_ref[...]
            s_ref[h] = s
            cmax_ref[h] = jnp.max(s, axis=0, keepdims=True)

    stage_scores(qi, True)

    def body(t, c):
        stage_values(qi - t + 2)
        stage_softmax()
        stage_scores(qi - t, False)
        return c

    lax.fori_loop(1, qi + 1, body, 0)

    stage_values(1)
    stage_softmax()
    stage_values(0)

    lam = (jnp.exp(jnp.sum(lq1_ref[...] * lk1_ref[...]))
           - jnp.exp(jnp.sum(lq2_ref[...] * lk2_ref[...])) + lambda_init)
    for hd in range(n_maps // 2):
        a1, a2 = acc_ref[2 * hd], acc_ref[2 * hd + 1]
        out_t = a1[:n_val] / a1[n_val:n_val + 1] - lam * (a2[:n_val] / a2[n_val:n_val + 1])
        ms = jnp.mean(out_t * out_t, axis=0, keepdims=True)
        out_t = out_t * lax.rsqrt(ms + EPS) * g_ref[...] * (1.0 - lambda_init)
        o_ref[0, :, hd * n_val:(hd + 1) * n_val] = out_t.T.astype(o_ref.dtype)


def _df_diagonal_bias(blk):
    chunk = jnp.arange(blk) // CHUNK
    return jnp.where(chunk[:, None] <= chunk[None, :], 0.0, NEG_BIG).astype(jnp.float32)


def _df_attention(proj3, lq1, lk1, lq2, lk2, g_col, mixed_cols, lambda_init):
    b, s, _ = proj3.shape
    blk = DF_QB
    cols = DF_HEADS * 2 * HEAD_DIM
    n_groups = mixed_cols // 2 // cols
    base = 3 * n_groups
    n_maps = 2 * DF_HEADS
    vec = pl.BlockSpec((1, HEAD_DIM), lambda bi, g, qi: (0, 0))
    return pl.pallas_call(
        functools.partial(_df_kernel, blk=blk, lambda_init=lambda_init),
        grid=(b, n_groups, s // blk),
        in_specs=[
            pl.BlockSpec((1, blk, cols), lambda bi, g, qi: (bi, qi, base + g)),
            pl.BlockSpec((1, s, cols), lambda bi, g, qi: (bi, 0, base + n_groups + g)),
            pl.BlockSpec((1, s, cols), lambda bi, g, qi: (bi, 0, base + 2 * n_groups + g)),
            pl.BlockSpec((blk, blk), lambda bi, g, qi: (0, 0)),
            vec, vec, vec, vec,
            pl.BlockSpec((2 * HEAD_DIM, 1), lambda bi, g, qi: (0, 0)),
        ],
        out_specs=pl.BlockSpec((1, blk, cols), lambda bi, g, qi: (bi, qi, g)),
        out_shape=jax.ShapeDtypeStruct((b, s, n_groups * cols), jnp.bfloat16),
        scratch_shapes=[
            pltpu.VMEM((n_maps, 2 * HEAD_DIM + SUM_ROWS, blk), jnp.float32),
            pltpu.VMEM((n_maps, 1, blk), jnp.float32),
            pltpu.VMEM((DF_HEADS, 2 * HEAD_DIM + SUM_ROWS, s), jnp.bfloat16),
            pltpu.VMEM((n_maps, blk, blk), jnp.float32),
            pltpu.VMEM((n_maps, blk, blk), jnp.bfloat16),
            pltpu.VMEM((n_maps, 1, blk), jnp.float32),
            pltpu.VMEM((n_maps, 1, blk), jnp.float32),
        ],
        compiler_params=pltpu.CompilerParams(
            dimension_semantics=("parallel", "parallel", "arbitrary"),
            vmem_limit_bytes=VMEM_LIMIT),
        name="df_attn",
    )(proj3, proj3, proj3, _df_diagonal_bias(blk), lq1, lk1, lq2, lk2, g_col)


def _mlp_kernel(x_ref, sb_ref, df_ref, wo_ref, g_ref, w1_ref, w2_ref, gf_ref, o_ref, *, final):
    half = sb_ref.shape[1]
    x1 = (x_ref[...] + _dot(sb_ref[...], wo_ref[:half, :]) + _dot(df_ref[...], wo_ref[half:, :]))
    h = _rmsnorm(x1, g_ref[...]).astype(jnp.bfloat16)
    y = None
    for c in range(w1_ref.shape[1] // FF_CHUNK):
        u = _dot(h, w1_ref[:, c * FF_CHUNK:(c + 1) * FF_CHUNK])
        u = jnp.square(jnp.maximum(u, 0.0)).astype(jnp.bfloat16)
        t = _dot(u, w2_ref[c * FF_CHUNK:(c + 1) * FF_CHUNK, :])
        y = t if y is None else y + t
    out = x1 + y
    if final:
        out = _rmsnorm(out, gf_ref[...])
    o_ref[...] = out


def _mlp(x2d, sb2d, df2d, wo, g, w1, w2, gf, final):
    n, d = x2d.shape
    half = sb2d.shape[1]
    ff = w1.shape[1]
    tm = MLP_TM
    const = lambda i: (0, 0)
    resident = dict(pipeline_mode=pl.Buffered(1))
    return pl.pallas_call(
        functools.partial(_mlp_kernel, final=final),
        grid=(n // tm,),
        in_specs=[
            pl.BlockSpec((tm, d), lambda i: (i, 0)),
            pl.BlockSpec((tm, half), lambda i: (i, 0)),
            pl.BlockSpec((tm, half), lambda i: (i, 0)),
            pl.BlockSpec((2 * half, d), const, **resident),
            pl.BlockSpec((1, d), const),
            pl.BlockSpec((d, ff), const, **resident),
            pl.BlockSpec((ff, d), const, **resident),
            pl.BlockSpec((1, d), const),
        ],
        out_specs=pl.BlockSpec((tm, d), lambda i: (i, 0)),
        out_shape=jax.ShapeDtypeStruct((n, d), jnp.float32),
        compiler_params=pltpu.CompilerParams(
            dimension_semantics=("parallel",), vmem_limit_bytes=VMEM_LIMIT),
        name="mlp",
    )(x2d, sb2d, df2d, wo, g, w1, w2, gf)


def _rope_tables(seq):
    half = HEAD_DIM // 2
    inv = 1.0 / (ROPE_THETA ** (jnp.arange(0, HEAD_DIM, 2, dtype=jnp.float32) / HEAD_DIM))
    ang = jnp.arange(seq, dtype=jnp.float32)[:, None] * inv[None, :]
    ang = jnp.concatenate([ang, ang], axis=-1)
    cos = jnp.cos(ang)
    sin = jnp.sin(ang)
    lo = jnp.arange(HEAD_DIM) < half
    sin_lo = jnp.where(lo, -sin, 0.0)
    sin_hi = jnp.where(lo, 0.0, sin)
    rep = LANES // HEAD_DIM
    return tuple(jnp.tile(t, (1, rep)) for t in (cos, sin_lo, sin_hi))


def kernel(x, w_in, w_o, attn_norm, subln_norm, lam_q1, lam_k1, lam_q2, lam_k2, mlp_norm, w_ff1, w_ff2, final_norm):
    b, s, d = x.shape
    depth = w_in.shape[0]
    mixed_cols = w_o.shape[1]
    assert s % PROJ_TM == 0 and (b * s) % MLP_TM == 0
    assert s % SB_QB == 0 and s % DF_QB == 0 and DF_QB % CHUNK == 0
    assert w_in.shape[2] == 3 * mixed_cols and subln_norm.shape[1] == LANES

    cos, sin_lo, sin_hi = _rope_tables(s)
    bf = jnp.bfloat16
    x2d = x.reshape(b * s, d)
    for l in range(depth):
        lambda_init = 0.8 - 0.6 * math.exp(-0.3 * l)
        proj = _proj(x2d, attn_norm[l][None, :], w_in[l].astype(bf), cos, sin_lo, sin_hi, s)
        proj3 = proj.reshape(b, s, -1)
        sb = _sb_attention(proj3, mixed_cols)
        df = _df_attention(proj3, lam_q1[l][None, :], lam_k1[l][None, :], lam_q2[l][None, :],
                           lam_k2[l][None, :], subln_norm[l][:, None], mixed_cols, lambda_init)
        x2d = _mlp(x2d, sb.reshape(b * s, -1), df.reshape(b * s, -1), w_o[l].astype(bf),
                   mlp_norm[l][None, :], w_ff1[l].astype(bf), w_ff2[l].astype(bf),
                   final_norm[None, :], final=(l == depth - 1))
    return x2d.reshape(b, s, d)
```

```python
import functools
import math

import jax
import jax.numpy as jnp
from jax import lax
from jax.experimental import pallas as pl
from jax.experimental.pallas import tpu as pltpu

EPS = 1e-6
ROPE_THETA = 10000.0
HEAD_DIM = 64
LANES = 128
CHUNK = 64
VMEM_LIMIT = 56 * 1024 * 1024

PROJ_TM = 512
MLP_TM = 512
FF_CHUNK = 1024
SB_QB = 256
SB_HEADS = 4
DF_QB = 512
DF_HEADS = 1
LOG2E = 1.4426950408889634
SB_SKIP_BELOW_LOG2 = -130.0
SUM_ROWS = 16
NEG_BIG = -1e30


def _dot(a, b):
    return jnp.dot(a, b, preferred_element_type=jnp.float32)


def _dot_nt(a, b):
    return lax.dot_general(a, b, (((1,), (1,)), ((), ())),
                           preferred_element_type=jnp.float32)


def _dot_tn(a, b):
    return lax.dot_general(a, b, (((0,), (0,)), ((), ())),
                           preferred_element_type=jnp.float32)


def _rmsnorm(x, g):
    return x * lax.rsqrt(jnp.mean(x * x, axis=-1, keepdims=True) + EPS) * g


def _proj_kernel(x_ref, g_ref, w_ref, cos_ref, sin_lo_ref, sin_hi_ref, o_ref, *, width):
    h = _rmsnorm(x_ref[...], g_ref[...]).astype(jnp.bfloat16)
    scale = HEAD_DIM ** -0.5
    group_scale = {0: scale * LOG2E, 3: scale}
    cos = cos_ref[...]
    sin_lo = sin_lo_ref[...]
    sin_hi = sin_hi_ref[...]
    for grp in range(6):
        p = _dot(h, w_ref[:, grp * width:(grp + 1) * width])
        rotary = grp in (3, 4)
        for c in range(width // LANES):
            t = p[:, c * LANES:(c + 1) * LANES]
            if rotary:
                t = (t * cos + pltpu.roll(t, LANES - HEAD_DIM // 2, 1) * sin_lo
                     + pltpu.roll(t, HEAD_DIM // 2, 1) * sin_hi)
            if grp in group_scale:
                t = t * group_scale[grp]
            o_ref[:, grp * width + c * LANES: grp * width + (c + 1) * LANES] = t.astype(o_ref.dtype)


def _proj(x2d, g, w, cos, sin_lo, sin_hi, seq):
    n, d = x2d.shape
    pw = w.shape[1]
    tm = PROJ_TM
    n_pos_blocks = seq // tm
    tab_spec = pl.BlockSpec((tm, LANES), lambda i: (i % n_pos_blocks, 0))
    return pl.pallas_call(
        functools.partial(_proj_kernel, width=pw // 6),
        grid=(n // tm,),
        in_specs=[
            pl.BlockSpec((tm, d), lambda i: (i, 0)),
            pl.BlockSpec((1, d), lambda i: (0, 0)),
            pl.BlockSpec((d, pw), lambda i: (0, 0), pipeline_mode=pl.Buffered(1)),
            tab_spec, tab_spec, tab_spec,
        ],
        out_specs=pl.BlockSpec((tm, pw), lambda i: (i, 0)),
        out_shape=jax.ShapeDtypeStruct((n, pw), jnp.bfloat16),
        compiler_params=pltpu.CompilerParams(
            dimension_semantics=("parallel",), vmem_limit_bytes=VMEM_LIMIT),
        name="proj",
    )(x2d, g, w, cos, sin_lo, sin_hi)


def _head_copies(q):
    lane = lax.broadcasted_iota(jnp.int32, q.shape, 1)
    zero = jnp.zeros_like(q)
    return [jnp.where((lane >= h * HEAD_DIM) & (lane < (h + 1) * HEAD_DIM), q, zero)
            for h in range(q.shape[1] // HEAD_DIM)]


def _sb_kernel(q_ref, k_ref, v_ref, tri_ref, o_ref,
               lb_ref, cat_ref, w_ref, carry_ref, extra_ref, *, blk):
    n_q = q_ref.shape[1] // blk
    heads = q_ref.shape[2] // HEAD_DIM
    key = lax.broadcasted_iota(jnp.int32, (blk, blk), 0)
    qry = lax.broadcasted_iota(jnp.int32, (blk, blk), 1)
    strict = key < qry

    def rows(ref, block):
        return ref[0, pl.ds(pl.multiple_of(block * blk, blk), blk), :]

    def window(ref, block):
        return jnp.concatenate([rows(ref, jnp.maximum(block - 1, 0)), rows(ref, block)], axis=0)

    def log_terms(zn2):
        neg_abs = lax.bitcast_convert_type(
            lax.bitcast_convert_type(zn2, jnp.uint32) | jnp.uint32(0x80000000), jnp.float32)
        log_keep = jnp.minimum(zn2, 0.0) - jnp.log2(1.0 + jnp.exp2(neg_abs))
        return log_keep, log_keep - zn2

    def split(log_keep):
        hi = lax.bitcast_convert_type(
            lax.bitcast_convert_type(log_keep, jnp.uint32) & jnp.uint32(0xFFFF0000), jnp.float32)
        lo = (log_keep - hi).astype(jnp.bfloat16)
        hi = hi.astype(jnp.bfloat16)
        return jnp.concatenate([hi, lo], axis=0)

    def suffix_sums(cat):
        return _dot(tri_ref[...], cat)

    lb_ref[...] = jnp.zeros_like(lb_ref)
    cat_ref[...] = jnp.zeros_like(cat_ref)
    w_ref[...] = jnp.zeros_like(w_ref)
    carry_ref[...] = jnp.zeros_like(carry_ref)
    extra_ref[...] = jnp.zeros_like(extra_ref)

    def step(t, p):
        cur = jnp.minimum(t, n_q - 1)
        fin = jnp.clip(t - 2, 0, n_q - 1)
        far = jnp.clip(t - 1, 0, n_q - 1)
        k_win = window(k_ref, cur)
        v_win = window(v_ref, fin)
        neg_q = [-c for c in _head_copies(rows(q_ref, cur))]
        no_left = jnp.where(t == 1, NEG_BIG, 0.0)
        outs = []
        for h in range(heads):
            log_keep, log_beta = log_terms(_dot_nt(k_win, neg_q[h]))
            lb_ref[p, h] = log_beta
            cat_ref[p, h, 0] = split(log_keep[:blk])
            cat_ref[p, h, 1] = split(jnp.where(strict, log_keep[blk:], 0.0))

            sums_diag = suffix_sums(cat_ref[1 - p, h, 1])
            w_diag = jnp.where(strict, jnp.exp2(lb_ref[1 - p, h, blk:] + sums_diag[:blk]), 0.0)
            sum_diag = sums_diag[blk:blk + 1]
            sums_left = suffix_sums(cat_ref[1 - p, h, 0])
            w_left = jnp.exp2(lb_ref[1 - p, h, :blk] + sums_left[:blk] + (sum_diag + no_left))
            w_ref[1 - p, h, :blk] = w_left.astype(w_ref.dtype)
            w_ref[1 - p, h, blk:] = w_diag.astype(w_ref.dtype)
            carry_ref[h] = sum_diag + sums_left[blk:blk + 1]
            extra_ref[1 - p, h] = jnp.zeros(extra_ref.shape[2:], extra_ref.dtype)

            v_h = v_win[:, h * HEAD_DIM:(h + 1) * HEAD_DIM]
            outs.append(_dot_tn(v_h, w_ref[p, h]) + extra_ref[p, h])
        out_t = jnp.concatenate(outs, axis=0)
        o_ref[0, pl.ds(pl.multiple_of(fin * blk, blk), blk), :] = out_t.T.astype(o_ref.dtype)

        def cond(state):
            j, max_carry = state
            return jnp.logical_and(j >= 0, max_carry > SB_SKIP_BELOW_LOG2)

        def body(state):
            j, _ = state
            k_far, v_far = rows(k_ref, j), rows(v_ref, j)
            neg_q_far = [-c for c in _head_copies(rows(q_ref, far))]
            for h in range(heads):
                carry = carry_ref[h]
                log_keep, log_beta = log_terms(_dot_nt(k_far, neg_q_far[h]))
                sums = suffix_sums(split(log_keep))
                w = jnp.exp2(log_beta + sums[:blk] + carry)
                extra_ref[1 - p, h] += _dot_tn(v_far[:, h * HEAD_DIM:(h + 1) * HEAD_DIM], w.astype(jnp.bfloat16))
                carry_ref[h] = carry + sums[blk:blk + 1]
            return j - 1, jnp.max(carry_ref[...])

        lax.while_loop(cond, body, (far - 2, jnp.max(carry_ref[...])))

    def two_steps(u, c):
        step(2 * u, 0)
        step(2 * u + 1, 1)
        return c

    lax.fori_loop(0, (n_q + 2) // 2, two_steps, 0)


def _sb_suffix_matrix(blk):
    later = (jnp.arange(blk)[None, :] > jnp.arange(blk)[:, None])
    later = jnp.concatenate([later, later], axis=1)
    every = jnp.ones((SUM_ROWS, 2 * blk), dtype=bool)
    return jnp.concatenate([later, every], axis=0).astype(jnp.bfloat16)


def _sb_attention(proj3, mixed_cols):
    b, s, _ = proj3.shape
    blk = SB_QB
    cols = SB_HEADS * HEAD_DIM
    n_groups = mixed_cols // 2 // cols
    return pl.pallas_call(
        functools.partial(_sb_kernel, blk=blk),
        grid=(b, n_groups),
        in_specs=[
            pl.BlockSpec((1, s, cols), lambda bi, g: (bi, 0, g)),
            pl.BlockSpec((1, s, cols), lambda bi, g: (bi, 0, n_groups + g)),
            pl.BlockSpec((1, s, cols), lambda bi, g: (bi, 0, 2 * n_groups + g)),
            pl.BlockSpec((blk + SUM_ROWS, 2 * blk), lambda bi, g: (0, 0)),
        ],
        out_specs=pl.BlockSpec((1, s, cols), lambda bi, g: (bi, 0, g)),
        out_shape=jax.ShapeDtypeStruct((b, s, n_groups * cols), jnp.bfloat16),
        scratch_shapes=[
            pltpu.VMEM((2, SB_HEADS, 2 * blk, blk), jnp.float32),
            pltpu.VMEM((2, SB_HEADS, 2, 2 * blk, blk), jnp.bfloat16),
            pltpu.VMEM((2, SB_HEADS, 2 * blk, blk), jnp.bfloat16),
            pltpu.VMEM((SB_HEADS, 1, blk), jnp.float32),
            pltpu.VMEM((2, SB_HEADS, HEAD_DIM, blk), jnp.float32),
        ],
        compiler_params=pltpu.CompilerParams(
            dimension_semantics=("parallel", "parallel"),
            vmem_limit_bytes=VMEM_LIMIT),
        name="sb_attn",
    )(proj3, proj3, proj3, _sb_suffix_matrix(blk))


def _df_kernel(q_ref, k_ref, v_ref, bias_ref, lq1_ref, lk1_ref, lq2_ref, lk2_ref, g_ref, o_ref,
               acc_ref, m_ref, vt_ref, s_ref, p_ref, alpha_ref, cmax_ref, *, blk, lambda_init):
    n_q = q_ref.shape[1] // blk
    n_maps = q_ref.shape[2] // HEAD_DIM
    n_val = 2 * HEAD_DIM

    for hd in range(n_maps // 2):
        for c in range(n_q):
            vt_ref[hd, :n_val, c * blk:(c + 1) * blk] = (
                v_ref[0, c * blk:(c + 1) * blk, hd * n_val:(hd + 1) * n_val].T)
    vt_ref[:, n_val:, :] = jnp.ones(
        (vt_ref.shape[0], vt_ref.shape[1] - n_val, vt_ref.shape[2]), vt_ref.dtype)

    lam = (jnp.exp(jnp.sum(lq1_ref[...] * lk1_ref[...]))
           - jnp.exp(jnp.sum(lq2_ref[...] * lk2_ref[...])) + lambda_init)

    def q_copies(qi):
        return _head_copies(q_ref[0, pl.ds(pl.multiple_of(qi * blk, blk), blk), :])

    def stage_values(j):
        vt_cols = pl.ds(pl.multiple_of(j * blk, blk), blk)
        for h in range(n_maps):
            acc_ref[h] = alpha_ref[h] * acc_ref[h] + _dot(vt_ref[h // 2, :, vt_cols], p_ref[h])

    def stage_softmax():
        for h in range(n_maps):
            s = s_ref[h]
            m_old = m_ref[h]
            m_new = jnp.maximum(m_old, cmax_ref[h])
            alpha = jnp.exp(m_old - m_new)
            p_ref[h] = jnp.exp(s - m_new).astype(p_ref.dtype)
            alpha_ref[h] = alpha
            m_ref[h] = m_new

    def stage_scores(j, q_maps, diagonal):
        k = k_ref[0, pl.ds(pl.multiple_of(j * blk, blk), blk), :]
        for h in range(n_maps):
            s = _dot_nt(k, q_maps[h])
            if diagonal:
                s = s + bias_ref[...]
            s_ref[h] = s
            cmax_ref[h] = jnp.max(s, axis=0, keepdims=True)

    def start_block():
        m_ref[...] = jnp.full_like(m_ref, NEG_BIG)

    def finish_block(qi):
        q_rows = pl.ds(pl.multiple_of(qi * blk, blk), blk)
        for hd in range(n_maps // 2):
            a1, a2 = acc_ref[2 * hd], acc_ref[2 * hd + 1]
            out_t = a1[:n_val] / a1[n_val:n_val + 1] - lam * (a2[:n_val] / a2[n_val:n_val + 1])
            ms = jnp.mean(out_t * out_t, axis=0, keepdims=True)
            out_t = out_t * lax.rsqrt(ms + EPS) * g_ref[...] * (1.0 - lambda_init)
            o_ref[0, q_rows, hd * n_val:(hd + 1) * n_val] = out_t.T.astype(o_ref.dtype)

    acc_ref[...] = jnp.zeros_like(acc_ref)
    start_block()
    stage_scores(0, q_copies(0), True)
    stage_softmax()
    stage_scores(1, q_copies(1), True)

    def query_block(qi, carry):
        q_maps = q_copies(qi)
        stage_values(0)
        finish_block(qi - 1)
        start_block()
        stage_softmax()
        stage_scores(qi - 1, q_maps, False)

        def body(t, c):
            stage_values(qi - t + 2)
            stage_softmax()
            stage_scores(qi - t, q_maps, False)
            return c

        lax.fori_loop(2, qi + 1, body, 0)

        stage_values(1)
        stage_softmax()
        nxt = jnp.minimum(qi + 1, n_q - 1)
        stage_scores(nxt, q_copies(nxt), True)
        return carry

    lax.fori_loop(1, n_q, query_block, 0)
    stage_values(0)
    finish_block(n_q - 1)


def _df_diagonal_bias(blk):
    chunk = jnp.arange(blk) // CHUNK
    return jnp.where(chunk[:, None] <= chunk[None, :], 0.0, NEG_BIG).astype(jnp.float32)


def _df_attention(proj3, lq1, lk1, lq2, lk2, g_col, mixed_cols, lambda_init):
    b, s, _ = proj3.shape
    blk = DF_QB
    cols = DF_HEADS * 2 * HEAD_DIM
    n_groups = mixed_cols // 2 // cols
    base = 3 * n_groups
    n_maps = 2 * DF_HEADS
    vec = pl.BlockSpec((1, HEAD_DIM), lambda bi, g: (0, 0))
    return pl.pallas_call(
        functools.partial(_df_kernel, blk=blk, lambda_init=lambda_init),
        grid=(b, n_groups),
        in_specs=[
            pl.BlockSpec((1, s, cols), lambda bi, g: (bi, 0, base + g)),
            pl.BlockSpec((1, s, cols), lambda bi, g: (bi, 0, base + n_groups + g)),
            pl.BlockSpec((1, s, cols), lambda bi, g: (bi, 0, base + 2 * n_groups + g)),
            pl.BlockSpec((blk, blk), lambda bi, g: (0, 0)),
            vec, vec, vec, vec,
            pl.BlockSpec((2 * HEAD_DIM, 1), lambda bi, g: (0, 0)),
        ],
        out_specs=pl.BlockSpec((1, s, cols), lambda bi, g: (bi, 0, g)),
        out_shape=jax.ShapeDtypeStruct((b, s, n_groups * cols), jnp.bfloat16),
        scratch_shapes=[
            pltpu.VMEM((n_maps, 2 * HEAD_DIM + SUM_ROWS, blk), jnp.float32),
            pltpu.VMEM((n_maps, 1, blk), jnp.float32),
            pltpu.VMEM((DF_HEADS, 2 * HEAD_DIM + SUM_ROWS, s), jnp.bfloat16),
            pltpu.VMEM((n_maps, blk, blk), jnp.float32),
            pltpu.VMEM((n_maps, blk, blk), jnp.bfloat16),
            pltpu.VMEM((n_maps, 1, blk), jnp.float32),
            pltpu.VMEM((n_maps, 1, blk), jnp.float32),
        ],
        compiler_params=pltpu.CompilerParams(
            dimension_semantics=("parallel", "parallel"),
            vmem_limit_bytes=VMEM_LIMIT),
        name="df_attn",
    )(proj3, proj3, proj3, _df_diagonal_bias(blk), lq1, lk1, lq2, lk2, g_col)


def _mlp_kernel(x_ref, sb_ref, df_ref, wo_ref, g_ref, w1_ref, w2_ref, gf_ref, o_ref, *, final):
    half = sb_ref.shape[1]
    x1 = (x_ref[...] + _dot(sb_ref[...], wo_ref[:half, :]) + _dot(df_ref[...], wo_ref[half:, :]))
    h = _rmsnorm(x1, g_ref[...]).astype(jnp.bfloat16)
    y = None
    for c in range(w1_ref.shape[1] // FF_CHUNK):
        u = _dot(h, w1_ref[:, c * FF_CHUNK:(c + 1) * FF_CHUNK])
        u = jnp.square(jnp.maximum(u, 0.0)).astype(jnp.bfloat16)
        t = _dot(u, w2_ref[c * FF_CHUNK:(c + 1) * FF_CHUNK, :])
        y = t if y is None else y + t
    out = x1 + y
    if final:
        out = _rmsnorm(out, gf_ref[...])
    o_ref[...] = out


def _mlp(x2d, sb2d, df2d, wo, g, w1, w2, gf, final):
    n, d = x2d.shape
    half = sb2d.shape[1]
    ff = w1.shape[1]
    tm = MLP_TM
    const = lambda i: (0, 0)
    resident = dict(pipeline_mode=pl.Buffered(1))
    return pl.pallas_call(
        functools.partial(_mlp_kernel, final=final),
        grid=(n // tm,),
        in_specs=[
            pl.BlockSpec((tm, d), lambda i: (i, 0)),
            pl.BlockSpec((tm, half), lambda i: (i, 0)),
            pl.BlockSpec((tm, half), lambda i: (i, 0)),
            pl.BlockSpec((2 * half, d), const, **resident),
            pl.BlockSpec((1, d), const),
            pl.BlockSpec((d, ff), const, **resident),
            pl.BlockSpec((ff, d), const, **resident),
            pl.BlockSpec((1, d), const),
        ],
        out_specs=pl.BlockSpec((tm, d), lambda i: (i, 0)),
        out_shape=jax.ShapeDtypeStruct((n, d), jnp.float32),
        compiler_params=pltpu.CompilerParams(
            dimension_semantics=("parallel",), vmem_limit_bytes=VMEM_LIMIT),
        name="mlp",
    )(x2d, sb2d, df2d, wo, g, w1, w2, gf)


def _rope_tables(seq):
    half = HEAD_DIM // 2
    inv = 1.0 / (ROPE_THETA ** (jnp.arange(0, HEAD_DIM, 2, dtype=jnp.float32) / HEAD_DIM))
    ang = jnp.arange(seq, dtype=jnp.float32)[:, None] * inv[None, :]
    ang = jnp.concatenate([ang, ang], axis=-1)
    cos = jnp.cos(ang)
    sin = jnp.sin(ang)
    lo = jnp.arange(HEAD_DIM) < half
    sin_lo = jnp.where(lo, -sin, 0.0)
    sin_hi = jnp.where(lo, 0.0, sin)
    rep = LANES // HEAD_DIM
    return tuple(jnp.tile(t, (1, rep)) for t in (cos, sin_lo, sin_hi))


def kernel(x, w_in, w_o, attn_norm, subln_norm, lam_q1, lam_k1, lam_q2, lam_k2, mlp_norm, w_ff1, w_ff2, final_norm):
    b, s, d = x.shape
    depth = w_in.shape[0]
    mixed_cols = w_o.shape[1]
    assert s % PROJ_TM == 0 and (b * s) % MLP_TM == 0
    assert s % SB_QB == 0 and s % DF_QB == 0 and DF_QB % CHUNK == 0
    assert w_in.shape[2] == 3 * mixed_cols and subln_norm.shape[1] == LANES

    cos, sin_lo, sin_hi = _rope_tables(s)
    bf = jnp.bfloat16
    x2d = x.reshape(b * s, d)
    for l in range(depth):
        lambda_init = 0.8 - 0.6 * math.exp(-0.3 * l)
        proj = _proj(x2d, attn_norm[l][None, :], w_in[l].astype(bf), cos, sin_lo, sin_hi, s)
        proj3 = proj.reshape(b, s, -1)
        sb = _sb_attention(proj3, mixed_cols)
        df = _df_attention(proj3, lam_q1[l][None, :], lam_k1[l][None, :], lam_q2[l][None, :],
                           lam_k2[l][None, :], subln_norm[l][:, None], mixed_cols, lambda_init)
        x2d = _mlp(x2d, sb.reshape(b * s, -1), df.reshape(b * s, -1), w_o[l].astype(bf),
                   mlp_norm[l][None, :], w_ff1[l].astype(bf), w_ff2[l].astype(bf),
                   final_norm[None, :], final=(l == depth - 1))
    return x2d.reshape(b, s, d)
```

```python
import functools
import math

import jax
import jax.numpy as jnp
from jax import lax
from jax.experimental import pallas as pl
from jax.experimental.pallas import tpu as pltpu

EPS = 1e-6
ROPE_THETA = 10000.0
HEAD_DIM = 64
LANES = 128
CHUNK = 64
VMEM_LIMIT = 56 * 1024 * 1024

PROJ_TM = 1024
MLP_TM = 1024
FF_CHUNK = 1024
SB_QB = 256
SB_HEADS = 4
DF_QB = 1024
DF_HEADS = 1
LOG2E = 1.4426950408889634
SB_SKIP_BELOW_LOG2 = -130.0
SUM_ROWS = 16
NEG_BIG = -1e30


def _dot(a, b):
    return jnp.dot(a, b, preferred_element_type=jnp.float32)


def _dot_nt(a, b):
    return lax.dot_general(a, b, (((1,), (1,)), ((), ())),
                           preferred_element_type=jnp.float32)


def _dot_tn(a, b):
    return lax.dot_general(a, b, (((0,), (0,)), ((), ())),
                           preferred_element_type=jnp.float32)


def _rmsnorm(x, g):
    return x * lax.rsqrt(jnp.mean(x * x, axis=-1, keepdims=True) + EPS) * g


def _proj_kernel(x_ref, g_ref, w_ref, cos_ref, sin_lo_ref, sin_hi_ref, o_ref, *, width):
    h = _rmsnorm(x_ref[...], g_ref[...]).astype(jnp.bfloat16)
    scale = HEAD_DIM ** -0.5
    group_scale = {0: scale * LOG2E, 3: scale}
    cos = cos_ref[...]
    sin_lo = sin_lo_ref[...]
    sin_hi = sin_hi_ref[...]
    for grp in range(6):
        p = _dot(h, w_ref[:, grp * width:(grp + 1) * width])
        rotary = grp in (3, 4)
        for c in range(width // LANES):
            t = p[:, c * LANES:(c + 1) * LANES]
            if rotary:
                t = (t * cos + pltpu.roll(t, LANES - HEAD_DIM // 2, 1) * sin_lo
                     + pltpu.roll(t, HEAD_DIM // 2, 1) * sin_hi)
            if grp in group_scale:
                t = t * group_scale[grp]
            o_ref[:, grp * width + c * LANES: grp * width + (c + 1) * LANES] = t.astype(o_ref.dtype)


def _proj(x2d, g, w, cos, sin_lo, sin_hi, seq):
    n, d = x2d.shape
    pw = w.shape[1]
    tm = PROJ_TM
    n_pos_blocks = seq // tm
    tab_spec = pl.BlockSpec((tm, LANES), lambda i: (i % n_pos_blocks, 0))
    return pl.pallas_call(
        functools.partial(_proj_kernel, width=pw // 6),
        grid=(n // tm,),
        in_specs=[
            pl.BlockSpec((tm, d), lambda i: (i, 0)),
            pl.BlockSpec((1, d), lambda i: (0, 0)),
            pl.BlockSpec((d, pw), lambda i: (0, 0), pipeline_mode=pl.Buffered(1)),
            tab_spec, tab_spec, tab_spec,
        ],
        out_specs=pl.BlockSpec((tm, pw), lambda i: (i, 0)),
        out_shape=jax.ShapeDtypeStruct((n, pw), jnp.bfloat16),
        compiler_params=pltpu.CompilerParams(
            dimension_semantics=("parallel",), vmem_limit_bytes=VMEM_LIMIT),
        name="proj",
    )(x2d, g, w, cos, sin_lo, sin_hi)


def _head_copies(q):
    lane = lax.broadcasted_iota(jnp.int32, q.shape, 1)
    zero = jnp.zeros_like(q)
    return [jnp.where((lane >= h * HEAD_DIM) & (lane < (h + 1) * HEAD_DIM), q, zero)
            for h in range(q.shape[1] // HEAD_DIM)]


def _sb_kernel(q_ref, k_ref, v_ref, tri_ref, o_ref,
               lb_ref, cat_ref, w_ref, carry_ref, extra_ref, *, blk):
    n_q = q_ref.shape[1] // blk
    heads = q_ref.shape[2] // HEAD_DIM
    key = lax.broadcasted_iota(jnp.int32, (blk, blk), 0)
    qry = lax.broadcasted_iota(jnp.int32, (blk, blk), 1)
    strict = key < qry

    def rows(ref, block):
        return ref[0, pl.ds(pl.multiple_of(block * blk, blk), blk), :]

    def window(ref, block):
        return jnp.concatenate([rows(ref, jnp.maximum(block - 1, 0)), rows(ref, block)], axis=0)

    def log_terms(zn2):
        log_keep = jnp.minimum(zn2, 0.0) - jnp.log2(1.0 + jnp.exp2(-jnp.abs(zn2)))
        return log_keep, log_keep - zn2

    def split(log_keep):
        hi = log_keep.astype(jnp.bfloat16)
        lo = (log_keep - hi.astype(jnp.float32)).astype(jnp.bfloat16)
        return jnp.concatenate([hi, lo], axis=0)

    def suffix_sums(cat):
        return _dot(tri_ref[...], cat)

    lb_ref[...] = jnp.zeros_like(lb_ref)
    cat_ref[...] = jnp.zeros_like(cat_ref)
    w_ref[...] = jnp.zeros_like(w_ref)
    carry_ref[...] = jnp.zeros_like(carry_ref)
    extra_ref[...] = jnp.zeros_like(extra_ref)

    def step(t, p):
        cur = jnp.minimum(t, n_q - 1)
        fin = jnp.clip(t - 2, 0, n_q - 1)
        far = jnp.clip(t - 1, 0, n_q - 1)
        k_win = window(k_ref, cur)
        v_win = window(v_ref, fin)
        neg_q = [-c for c in _head_copies(rows(q_ref, cur))]
        no_left = jnp.where(t == 1, NEG_BIG, 0.0)
        outs = []
        for h in range(heads):
            log_keep, log_beta = log_terms(_dot_nt(k_win, neg_q[h]))
            lb_ref[p, h] = log_beta
            cat_ref[p, h, 0] = split(log_keep[:blk])
            cat_ref[p, h, 1] = split(jnp.where(strict, log_keep[blk:], 0.0))

            sums_diag = suffix_sums(cat_ref[1 - p, h, 1])
            w_diag = jnp.where(strict, jnp.exp2(lb_ref[1 - p, h, blk:] + sums_diag[:blk]), 0.0)
            sum_diag = sums_diag[blk:blk + 1]
            sums_left = suffix_sums(cat_ref[1 - p, h, 0])
            w_left = jnp.exp2(lb_ref[1 - p, h, :blk] + sums_left[:blk] + (sum_diag + no_left))
            w_ref[1 - p, h, :blk] = w_left.astype(w_ref.dtype)
            w_ref[1 - p, h, blk:] = w_diag.astype(w_ref.dtype)
            carry_ref[h] = sum_diag + sums_left[blk:blk + 1]
            extra_ref[1 - p, h] = jnp.zeros(extra_ref.shape[2:], extra_ref.dtype)

            v_h = v_win[:, h * HEAD_DIM:(h + 1) * HEAD_DIM]
            outs.append(_dot_tn(v_h, w_ref[p, h]) + extra_ref[p, h])
        out_t = jnp.concatenate(outs, axis=0)
        o_ref[0, pl.ds(pl.multiple_of(fin * blk, blk), blk), :] = out_t.T.astype(o_ref.dtype)

        def cond(state):
            j, max_carry = state
            return jnp.logical_and(j >= 0, max_carry > SB_SKIP_BELOW_LOG2)

        def body(state):
            j, _ = state
            k_far, v_far = rows(k_ref, j), rows(v_ref, j)
            neg_q_far = [-c for c in _head_copies(rows(q_ref, far))]
            for h in range(heads):
                carry = carry_ref[h]
                log_keep, log_beta = log_terms(_dot_nt(k_far, neg_q_far[h]))
                sums = suffix_sums(split(log_keep))
                w = jnp.exp2(log_beta + sums[:blk] + carry)
                extra_ref[1 - p, h] += _dot_tn(v_far[:, h * HEAD_DIM:(h + 1) * HEAD_DIM], w.astype(jnp.bfloat16))
                carry_ref[h] = carry + sums[blk:blk + 1]
            return j - 1, jnp.max(carry_ref[...])

        lax.while_loop(cond, body, (far - 2, jnp.max(carry_ref[...])))

    def two_steps(u, c):
        step(2 * u, 0)
        step(2 * u + 1, 1)
        return c

    lax.fori_loop(0, (n_q + 2) // 2, two_steps, 0)


def _sb_suffix_matrix(blk):
    later = (jnp.arange(blk)[None, :] > jnp.arange(blk)[:, None])
    later = jnp.concatenate([later, later], axis=1)
    every = jnp.ones((SUM_ROWS, 2 * blk), dtype=bool)
    return jnp.concatenate([later, every], axis=0).astype(jnp.bfloat16)


def _sb_attention(proj3, mixed_cols):
    b, s, _ = proj3.shape
    blk = SB_QB
    cols = SB_HEADS * HEAD_DIM
    n_groups = mixed_cols // 2 // cols
    return pl.pallas_call(
        functools.partial(_sb_kernel, blk=blk),
        grid=(b, n_groups),
        in_specs=[
            pl.BlockSpec((1, s, cols), lambda bi, g: (bi, 0, g)),
            pl.BlockSpec((1, s, cols), lambda bi, g: (bi, 0, n_groups + g)),
            pl.BlockSpec((1, s, cols), lambda bi, g: (bi, 0, 2 * n_groups + g)),
            pl.BlockSpec((blk + SUM_ROWS, 2 * blk), lambda bi, g: (0, 0)),
        ],
        out_specs=pl.BlockSpec((1, s, cols), lambda bi, g: (bi, 0, g)),
        out_shape=jax.ShapeDtypeStruct((b, s, n_groups * cols), jnp.bfloat16),
        scratch_shapes=[
            pltpu.VMEM((2, SB_HEADS, 2 * blk, blk), jnp.float32),
            pltpu.VMEM((2, SB_HEADS, 2, 2 * blk, blk), jnp.bfloat16),
            pltpu.VMEM((2, SB_HEADS, 2 * blk, blk), jnp.bfloat16),
            pltpu.VMEM((SB_HEADS, 1, blk), jnp.float32),
            pltpu.VMEM((2, SB_HEADS, HEAD_DIM, blk), jnp.float32),
        ],
        compiler_params=pltpu.CompilerParams(
            dimension_semantics=("parallel", "parallel"),
            vmem_limit_bytes=VMEM_LIMIT),
        name="sb_attn",
    )(proj3, proj3, proj3, _sb_suffix_matrix(blk))


def _df_kernel(q_ref, k_ref, v_ref, bias_ref, lq1_ref, lk1_ref, lq2_ref, lk2_ref, g_ref, o_ref,
               acc_ref, m_ref, vt_ref, s_ref, p_ref, alpha_ref, cmax_ref, *, blk, lambda_init):
    n_q = q_ref.shape[1] // blk
    n_maps = q_ref.shape[2] // HEAD_DIM
    n_val = 2 * HEAD_DIM

    for hd in range(n_maps // 2):
        for c in range(n_q):
            vt_ref[hd, :n_val, c * blk:(c + 1) * blk] = (
                v_ref[0, c * blk:(c + 1) * blk, hd * n_val:(hd + 1) * n_val].T)
    vt_ref[:, n_val:, :] = jnp.ones(
        (vt_ref.shape[0], vt_ref.shape[1] - n_val, vt_ref.shape[2]), vt_ref.dtype)

    lam = (jnp.exp(jnp.sum(lq1_ref[...] * lk1_ref[...]))
           - jnp.exp(jnp.sum(lq2_ref[...] * lk2_ref[...])) + lambda_init)

    def q_copies(qi):
        return _head_copies(q_ref[0, pl.ds(pl.multiple_of(qi * blk, blk), blk), :])

    def stage_values(j):
        vt_cols = pl.ds(pl.multiple_of(j * blk, blk), blk)
        for h in range(n_maps):
            acc_ref[h] = alpha_ref[h] * acc_ref[h] + _dot(vt_ref[h // 2, :, vt_cols], p_ref[h])

    def stage_softmax():
        for h in range(n_maps):
            s = s_ref[h]
            m_old = m_ref[h]
            m_new = jnp.maximum(m_old, cmax_ref[h])
            alpha = jnp.exp(m_old - m_new)
            p_ref[h] = jnp.exp(s - m_new).astype(p_ref.dtype)
            alpha_ref[h] = alpha
            m_ref[h] = m_new

    def stage_scores(j, q_maps, diagonal):
        k = k_ref[0, pl.ds(pl.multiple_of(j * blk, blk), blk), :]
        for h in range(n_maps):
            s = _dot_nt(k, q_maps[h])
            if diagonal:
                s = s + bias_ref[...]
            s_ref[h] = s
            cmax_ref[h] = jnp.max(s, axis=0, keepdims=True)

    def start_block():
        m_ref[...] = jnp.full_like(m_ref, NEG_BIG)

    def finish_block(qi):
        q_rows = pl.ds(pl.multiple_of(qi * blk, blk), blk)
        for hd in range(n_maps // 2):
            a1, a2 = acc_ref[2 * hd], acc_ref[2 * hd + 1]
            out_t = a1[:n_val] / a1[n_val:n_val + 1] - lam * (a2[:n_val] / a2[n_val:n_val + 1])
            ms = jnp.mean(out_t * out_t, axis=0, keepdims=True)
            out_t = out_t * lax.rsqrt(ms + EPS) * g_ref[...] * (1.0 - lambda_init)
            o_ref[0, q_rows, hd * n_val:(hd + 1) * n_val] = out_t.T.astype(o_ref.dtype)

    acc_ref[...] = jnp.zeros_like(acc_ref)
    start_block()
    stage_scores(0, q_copies(0), True)
    stage_softmax()
    stage_scores(1, q_copies(1), True)

    def query_block(qi, carry):
        q_maps = q_copies(qi)
        stage_values(0)
        finish_block(qi - 1)
        start_block()
        stage_softmax()
        stage_scores(qi - 1, q_maps, False)

        def body(t, c):
            stage_values(qi - t + 2)
            stage_softmax()
            stage_scores(qi - t, q_maps, False)
            return c

        lax.fori_loop(2, qi + 1, body, 0)

        stage_values(1)
        stage_softmax()
        nxt = jnp.minimum(qi + 1, n_q - 1)
        stage_scores(nxt, q_copies(nxt), True)
        return carry

    lax.fori_loop(1, n_q, query_block, 0)
    stage_values(0)
    finish_block(n_q - 1)


def _df_diagonal_bias(blk):
    chunk = jnp.arange(blk) // CHUNK
    return jnp.where(chunk[:, None] <= chunk[None, :], 0.0, NEG_BIG).astype(jnp.float32)


def _df_attention(proj3, lq1, lk1, lq2, lk2, g_col, mixed_cols, lambda_init):
    b, s, _ = proj3.shape
    blk = DF_QB
    cols = DF_HEADS * 2 * HEAD_DIM
    n_groups = mixed_cols // 2 // cols
    base = 3 * n_groups
    n_maps = 2 * DF_HEADS
    vec = pl.BlockSpec((1, HEAD_DIM), lambda bi, g: (0, 0))
    return pl.pallas_call(
        functools.partial(_df_kernel, blk=blk, lambda_init=lambda_init),
        grid=(b, n_groups),
        in_specs=[
            pl.BlockSpec((1, s, cols), lambda bi, g: (bi, 0, base + g)),
            pl.BlockSpec((1, s, cols), lambda bi, g: (bi, 0, base + n_groups + g)),
            pl.BlockSpec((1, s, cols), lambda bi, g: (bi, 0, base + 2 * n_groups + g)),
            pl.BlockSpec((blk, blk), lambda bi, g: (0, 0)),
            vec, vec, vec, vec,
            pl.BlockSpec((2 * HEAD_DIM, 1), lambda bi, g: (0, 0)),
        ],
        out_specs=pl.BlockSpec((1, s, cols), lambda bi, g: (bi, 0, g)),
        out_shape=jax.ShapeDtypeStruct((b, s, n_groups * cols), jnp.bfloat16),
        scratch_shapes=[
            pltpu.VMEM((n_maps, 2 * HEAD_DIM + SUM_ROWS, blk), jnp.float32),
            pltpu.VMEM((n_maps, 1, blk), jnp.float32),
            pltpu.VMEM((DF_HEADS, 2 * HEAD_DIM + SUM_ROWS, s), jnp.bfloat16),
            pltpu.VMEM((n_maps, blk, blk), jnp.float32),
            pltpu.VMEM((n_maps, blk, blk), jnp.bfloat16),
            pltpu.VMEM((n_maps, 1, blk), jnp.float32),
            pltpu.VMEM((n_maps, 1, blk), jnp.float32),
        ],
        compiler_params=pltpu.CompilerParams(
            dimension_semantics=("parallel", "parallel"),
            vmem_limit_bytes=VMEM_LIMIT),
        name="df_attn",
    )(proj3, proj3, proj3, _df_diagonal_bias(blk), lq1, lk1, lq2, lk2, g_col)


def _mlp_kernel(x_ref, sb_ref, df_ref, wo_ref, g_ref, w1_ref, w2_ref, gf_ref, o_ref, *, final):
    half = sb_ref.shape[1]
    x1 = (x_ref[...] + _dot(sb_ref[...], wo_ref[:half, :]) + _dot(df_ref[...], wo_ref[half:, :]))
    h = _rmsnorm(x1, g_ref[...]).astype(jnp.bfloat16)
    y = None
    for c in range(w1_ref.shape[1] // FF_CHUNK):
        u = _dot(h, w1_ref[:, c * FF_CHUNK:(c + 1) * FF_CHUNK])
        u = jnp.square(jnp.maximum(u, 0.0)).astype(jnp.bfloat16)
        t = _dot(u, w2_ref[c * FF_CHUNK:(c + 1) * FF_CHUNK, :])
        y = t if y is None else y + t
    out = x1 + y
    if final:
        out = _rmsnorm(out, gf_ref[...])
    o_ref[...] = out


def _mlp(x2d, sb2d, df2d, wo, g, w1, w2, gf, final):
    n, d = x2d.shape
    half = sb2d.shape[1]
    ff = w1.shape[1]
    tm = MLP_TM
    const = lambda i: (0, 0)
    resident = dict(pipeline_mode=pl.Buffered(1))
    return pl.pallas_call(
        functools.partial(_mlp_kernel, final=final),
        grid=(n // tm,),
        in_specs=[
            pl.BlockSpec((tm, d), lambda i: (i, 0)),
            pl.BlockSpec((tm, half), lambda i: (i, 0)),
            pl.BlockSpec((tm, half), lambda i: (i, 0)),
            pl.BlockSpec((2 * half, d), const, **resident),
            pl.BlockSpec((1, d), const),
            pl.BlockSpec((d, ff), const, **resident),
            pl.BlockSpec((ff, d), const, **resident),
            pl.BlockSpec((1, d), const),
        ],
        out_specs=pl.BlockSpec((tm, d), lambda i: (i, 0)),
        out_shape=jax.ShapeDtypeStruct((n, d), jnp.float32),
        compiler_params=pltpu.CompilerParams(
            dimension_semantics=("parallel",), vmem_limit_bytes=VMEM_LIMIT),
        name="mlp",
    )(x2d, sb2d, df2d, wo, g, w1, w2, gf)


def _rope_tables(seq):
    half = HEAD_DIM // 2
    inv = 1.0 / (ROPE_THETA ** (jnp.arange(0, HEAD_DIM, 2, dtype=jnp.float32) / HEAD_DIM))
    ang = jnp.arange(seq, dtype=jnp.float32)[:, None] * inv[None, :]
    ang = jnp.concatenate([ang, ang], axis=-1)
    cos = jnp.cos(ang)
    sin = jnp.sin(ang)
    lo = jnp.arange(HEAD_DIM) < half
    sin_lo = jnp.where(lo, -sin, 0.0)
    sin_hi = jnp.where(lo, 0.0, sin)
    rep = LANES // HEAD_DIM
    return tuple(jnp.tile(t, (1, rep)) for t in (cos, sin_lo, sin_hi))


def kernel(x, w_in, w_o, attn_norm, subln_norm, lam_q1, lam_k1, lam_q2, lam_k2, mlp_norm, w_ff1, w_ff2, final_norm):
    b, s, d = x.shape
    depth = w_in.shape[0]
    mixed_cols = w_o.shape[1]
    assert s % PROJ_TM == 0 and (b * s) % MLP_TM == 0
    assert s % SB_QB == 0 and s % DF_QB == 0 and DF_QB % CHUNK == 0
    assert s // DF_QB >= 2 and (s // SB_QB) % 2 == 0
    assert w_in.shape[2] == 3 * mixed_cols and subln_norm.shape[1] == LANES

    cos, sin_lo, sin_hi = _rope_tables(s)
    bf = jnp.bfloat16
    x2d = x.reshape(b * s, d)
    for l in range(depth):
        lambda_init = 0.8 - 0.6 * math.exp(-0.3 * l)
        proj = _proj(x2d, attn_norm[l][None, :], w_in[l].astype(bf), cos, sin_lo, sin_hi, s)
        proj3 = proj.reshape(b, s, -1)
        sb = _sb_attention(proj3, mixed_cols)
        df = _df_attention(proj3, lam_q1[l][None, :], lam_k1[l][None, :], lam_q2[l][None, :],
                           lam_k2[l][None, :], subln_norm[l][:, None], mixed_cols, lambda_init)
        x2d = _mlp(x2d, sb.reshape(b * s, -1), df.reshape(b * s, -1), w_o[l].astype(bf),
                   mlp_norm[l][None, :], w_ff1[l].astype(bf), w_ff2[l].astype(bf),
                   final_norm[None, :], final=(l == depth - 1))
    return x2d.reshape(b, s, d)
```
